```python
import jax, jax.numpy as jnp
from jax import lax
import numpy as np

D_MODEL = 1024
BATCH = 2
SEQ = 8192
DEPTH = 4
DEC_BATCH = 32
DEC_SEQ = 8
PAST_LEN = 8192
PAGE_SIZE = 128

N_A_LAYERS = DEPTH // 2
N_B_LAYERS = DEPTH - N_A_LAYERS
N_HEADS = 16
HEAD_DIM = D_MODEL // N_HEADS
CONV_WIDTH = 3
D_FF = -(-8 * D_MODEL // (3 * 256)) * 256
PLE_DIM = 256
Q_BLOCK = 128
EPS = 1e-6
FORGET_BIAS_INIT = 3.0

kernel_name = "yoco_shortconv_forgetting_attn_step"


def rmsnorm(x, g):
    xf = x.astype(jnp.float32)
    y = xf * lax.rsqrt(jnp.mean(xf * xf, axis=-1, keepdims=True) + EPS) * g.astype(jnp.float32)
    return y.astype(x.dtype)


def swiglu(x, w_gate, w_up, w_down):
    return (jax.nn.silu(x @ w_gate) * (x @ w_up)) @ w_down


def short_conv(x, w_in, w_conv, w_out, buf):
    b_gate, c_gate, u = jnp.split(x @ w_in, 3, axis=-1)
    cu = c_gate * u
    t = x.shape[1]
    padded = jnp.concatenate([buf.astype(cu.dtype), cu], axis=1)
    conv = w_conv[0] * padded[:, 0:t]
    for k in range(1, CONV_WIDTH):
        conv = conv + w_conv[k] * padded[:, k:k + t]
    y = (b_gate * conv) @ w_out
    return y, padded[:, -(CONV_WIDTH - 1):]


def shared_kv(h, g_kv, w_k, w_v, w_f, b_f):
    b, t, _ = h.shape
    z = rmsnorm(h, g_kv)
    k = (z @ w_k).reshape(b, t, N_HEADS, HEAD_DIM)
    v = (z @ w_v).reshape(b, t, N_HEADS, HEAD_DIM)
    logf = jax.nn.log_sigmoid((z @ w_f + b_f).astype(jnp.float32))
    return k, v, logf


def forgetting_attn_prompt(q, k, v, logf):
    b, s_len, h, dh = q.shape
    f_cum = jnp.transpose(jnp.cumsum(logf.astype(jnp.float32), axis=1), (0, 2, 1))
    key_pos = jnp.arange(s_len)
    scale = HEAD_DIM ** -0.5

    def block(i):
        start = i * Q_BLOCK
        qb = lax.dynamic_slice_in_dim(q, start, Q_BLOCK, axis=1)
        fq = lax.dynamic_slice_in_dim(f_cum, start, Q_BLOCK, axis=2)
        logits = (jnp.einsum('bqhd,bkhd->bhqk', qb, k).astype(jnp.float32) * scale
                  + (fq[..., None] - f_cum[:, :, None, :]))
        qpos = start + jnp.arange(Q_BLOCK)
        logits = jnp.where(qpos[:, None] >= key_pos[None, :], logits, -jnp.inf)
        probs = jax.nn.softmax(logits, axis=-1).astype(v.dtype)
        return jnp.einsum('bhqk,bkhd->bqhd', probs, v)

    out = lax.map(block, jnp.arange(s_len // Q_BLOCK))
    return jnp.moveaxis(out, 0, 1).reshape(b, s_len, h * dh)


def forgetting_attn_sample(q, k_past, v_past, logf_past, k_new, v_new, logf_new):
    bd, t, h, dh = q.shape
    scale = HEAD_DIM ** -0.5
    f_past = jnp.cumsum(logf_past.astype(jnp.float32), axis=1)
    f_new = f_past[:, -1:] + jnp.cumsum(logf_new.astype(jnp.float32), axis=1)
    fq = jnp.transpose(f_new, (0, 2, 1))[..., None]
    s_past = (jnp.einsum('bqhd,bkhd->bhqk', q, k_past).astype(jnp.float32) * scale
              + fq - jnp.transpose(f_past, (0, 2, 1))[:, :, None, :])
    s_new = (jnp.einsum('bqhd,bkhd->bhqk', q, k_new).astype(jnp.float32) * scale
             + fq - jnp.transpose(f_new, (0, 2, 1))[:, :, None, :])
    causal = jnp.arange(t)[:, None] >= jnp.arange(t)[None, :]
    s_new = jnp.where(causal, s_new, -jnp.inf)
    p_len = k_past.shape[1]
    probs = jax.nn.softmax(jnp.concatenate([s_past, s_new], axis=-1), axis=-1).astype(v_new.dtype)
    out = (jnp.einsum('bhqk,bkhd->bqhd', probs[..., :p_len], v_past)
           + jnp.einsum('bhqk,bkhd->bqhd', probs[..., p_len:], v_new))
    return out.reshape(bd, t, h * dh)


def run_trunk(h, p, conv_bufs, past, norm_g, w_in, w_conv, w_out, g_kv, w_k, w_v, w_f, b_f,
              w_q, w_o, w_gate, w_up, w_down, w_ple, w_ple_gate):
    b, t, _ = h.shape
    new_bufs = []
    kv_rows = None
    for i in range(DEPTH):
        xn = rmsnorm(h, norm_g[i, 0])
        if i < N_A_LAYERS:
            y, buf = short_conv(xn, w_in[i], w_conv[i], w_out[i], conv_bufs[i])
            new_bufs.append(buf)
        else:
            j = i - N_A_LAYERS
            if kv_rows is None:
                kv_rows = shared_kv(h, g_kv, w_k, w_v, w_f, b_f)
            q = (xn @ w_q[j]).reshape(b, t, N_HEADS, HEAD_DIM)
            if past is None:
                o = forgetting_attn_prompt(q, kv_rows[0], kv_rows[1], kv_rows[2])
            else:
                o = forgetting_attn_sample(q, past[0], past[1], past[2], kv_rows[0], kv_rows[1], kv_rows[2])
            y = o @ w_o[j]
        h = h + rmsnorm(y, norm_g[i, 1])
        h = h + rmsnorm(swiglu(rmsnorm(h, norm_g[i, 2]), w_gate[i], w_up[i], w_down[i]), norm_g[i, 3])
        h = h + jax.nn.sigmoid(h @ w_ple_gate[i]) * (p[i] @ w_ple[i])
    return h, jnp.stack(new_bufs, axis=0), kv_rows


def setup_inputs(seed: int = 0) -> dict:
    key = jax.random.key(seed)
    ks = jax.random.split(key, 28)
    n_pages = PAST_LEN // PAGE_SIZE
    n_used = DEC_BATCH * n_pages
    n_pool = n_used + n_used // 4

    def nrm(k, shape, scale):
        return jax.random.normal(k, shape, jnp.float32) * scale

    return {
        'x_prompt': nrm(ks[0], (BATCH, SEQ, D_MODEL), 1.0),
        'x_sample': nrm(ks[1], (DEC_BATCH, DEC_SEQ, D_MODEL), 1.0),
        'state_conv': nrm(ks[2], (N_A_LAYERS, DEC_BATCH, CONV_WIDTH - 1, D_MODEL), 1.0),
        'cache_k': nrm(ks[3], (n_pool, PAGE_SIZE, N_HEADS, HEAD_DIM), 1.0),
        'cache_v': nrm(ks[4], (n_pool, PAGE_SIZE, N_HEADS, HEAD_DIM), 1.0),
        'cache_logf': jax.nn.log_sigmoid(FORGET_BIAS_INIT + nrm(ks[5], (n_pool, PAGE_SIZE, N_HEADS), 1.0)),
        'page_table': jax.random.permutation(ks[6], n_pool)[:n_used].reshape(DEC_BATCH, n_pages).astype(jnp.int32),
        'p_prompt': nrm(ks[7], (DEPTH, BATCH, SEQ, PLE_DIM), 1.0),
        'p_sample': nrm(ks[8], (DEPTH, DEC_BATCH, DEC_SEQ, PLE_DIM), 1.0),
        'norm_g': 1.0 + nrm(ks[9], (DEPTH, 4, D_MODEL), 0.02),
        'w_in': nrm(ks[10], (N_A_LAYERS, D_MODEL, 3 * D_MODEL), D_MODEL ** -0.5),
        'w_conv': nrm(ks[11], (N_A_LAYERS, CONV_WIDTH, D_MODEL), CONV_WIDTH ** -0.5),
        'w_out': nrm(ks[12], (N_A_LAYERS, D_MODEL, D_MODEL), D_MODEL ** -0.5),
        'g_kv': 1.0 + nrm(ks[13], (D_MODEL,), 0.02),
        'w_k': nrm(ks[14], (D_MODEL, D_MODEL), D_MODEL ** -0.5),
        'w_v': nrm(ks[15], (D_MODEL, D_MODEL), D_MODEL ** -0.5),
        'w_f': nrm(ks[16], (D_MODEL, N_HEADS), D_MODEL ** -0.5),
        'b_f': FORGET_BIAS_INIT + nrm(ks[17], (N_HEADS,), 0.1),
        'w_q': nrm(ks[18], (N_B_LAYERS, D_MODEL, D_MODEL), D_MODEL ** -0.5),
        'w_o': nrm(ks[19], (N_B_LAYERS, D_MODEL, D_MODEL), D_MODEL ** -0.5),
        'w_gate': nrm(ks[20], (DEPTH, D_MODEL, D_FF), D_MODEL ** -0.5),
        'w_up': nrm(ks[21], (DEPTH, D_MODEL, D_FF), D_MODEL ** -0.5),
        'w_down': nrm(ks[22], (DEPTH, D_FF, D_MODEL), D_FF ** -0.5),
        'w_ple': nrm(ks[23], (DEPTH, PLE_DIM, D_MODEL), PLE_DIM ** -0.5),
        'w_ple_gate': nrm(ks[24], (DEPTH, D_MODEL, D_MODEL), D_MODEL ** -0.5),
    }


def reference(x_prompt, x_sample, state_conv, cache_k, cache_v, cache_logf, page_table, p_prompt, p_sample,
              norm_g, w_in, w_conv, w_out, g_kv, w_k, w_v, w_f, b_f, w_q, w_o,
              w_gate, w_up, w_down, w_ple, w_ple_gate):
    bd = x_sample.shape[0]
    n_pages = page_table.shape[1]
    conv_zero = jnp.zeros((N_A_LAYERS, x_prompt.shape[0], CONV_WIDTH - 1, D_MODEL), x_prompt.dtype)
    y_prompt, conv_prompt, kv_prompt = run_trunk(
        x_prompt, p_prompt, conv_zero, None, norm_g, w_in, w_conv, w_out, g_kv, w_k, w_v, w_f, b_f,
        w_q, w_o, w_gate, w_up, w_down, w_ple, w_ple_gate)
    k_past = cache_k[page_table].reshape(bd, n_pages * PAGE_SIZE, N_HEADS, HEAD_DIM)
    v_past = cache_v[page_table].reshape(bd, n_pages * PAGE_SIZE, N_HEADS, HEAD_DIM)
    logf_past = cache_logf[page_table].reshape(bd, n_pages * PAGE_SIZE, N_HEADS)
    y_sample, conv_sample, kv_sample = run_trunk(
        x_sample, p_sample, state_conv, (k_past, v_past, logf_past), norm_g, w_in, w_conv, w_out,
        g_kv, w_k, w_v, w_f, b_f, w_q, w_o, w_gate, w_up, w_down, w_ple, w_ple_gate)
    return (y_prompt, y_sample, conv_prompt, kv_prompt[0], kv_prompt[1], kv_prompt[2],
            conv_sample, kv_sample[0], kv_sample[1], kv_sample[2])
```

```python
import functools

import jax
import jax.numpy as jnp
import numpy as np
from jax import lax
from jax.experimental import pallas as pl
from jax.experimental.pallas import tpu as pltpu

D_MODEL = 1024
N_HEADS = 16
HEAD_DIM = 64
D_FF = 2816
PLE_DIM = 256
EPS = 1e-6
N_CONV_LAYERS = 2
PAGE = 128
LANES = 128
SUBLANES = 8
SAMPLE_SEQ = 8
HEAD_SLOT = 128
BIAS_LANE = HEAD_DIM
SCALE = HEAD_DIM ** -0.5

ROW_TILE = 512
FFN_TILE = 256
Q_TILE = 512
HEAD_GROUP = 4
PAGES_PER_STEP = 4
PREPASS_PAGES = 128
VMEM_LIMIT = 56 * 1024 * 1024

_F32 = jnp.float32
_BF16 = jnp.bfloat16


def _rms(x, g):
    return x * lax.rsqrt(jnp.mean(x * x, axis=-1, keepdims=True) + EPS) * g


def _dot(a, b):
    return jnp.dot(a, b, preferred_element_type=_F32)


def _dot_nt(a, b):
    return lax.dot_general(a, b, (((1,), (1,)), ((), ())), preferred_element_type=_F32)


def _split3(x):
    hi = x.astype(_BF16)
    r = x - hi.astype(_F32)
    mid = r.astype(_BF16)
    lo = (r - mid.astype(_F32)).astype(_BF16)
    return hi, mid, lo


def _const_spec(shape):
    nd = len(shape)
    return pl.BlockSpec(shape, lambda *_: (0,) * nd, pipeline_mode=pl.Buffered(1))


def _params(sem):
    return pltpu.CompilerParams(dimension_semantics=sem, vmem_limit_bytes=VMEM_LIMIT)


def _conv_body(sample, *refs):
    if sample:
        (h_ref, fix1_ref, fix2_ref, g_ref, win_ref, wconv_ref, wout_ref, out_ref, cu_ref) = refs
    else:
        (h_ref, g_ref, win_ref, wconv_ref, wout_ref, out_ref, state_ref, carry_ref) = refs

        @pl.when(pl.program_id(1) == 0)
        def _():
            carry_ref[...] = jnp.zeros_like(carry_ref)

    x = h_ref[...].reshape(h_ref.shape[-2:])
    tm = x.shape[0]
    xn = _rms(x, g_ref[0:1, :]).astype(_BF16)
    proj = _dot(xn, win_ref[...])
    bg = proj[:, :D_MODEL]
    cu = proj[:, D_MODEL:2 * D_MODEL] * proj[:, 2 * D_MODEL:]
    s1 = pltpu.roll(cu, 1, 0)
    s2 = pltpu.roll(cu, 2, 0)
    row = lax.broadcasted_iota(jnp.int32, (tm, 1), 0)
    if sample:
        pos = row % SAMPLE_SEQ
        s1 = jnp.where(pos == 0, fix1_ref[...], s1)
        s2 = jnp.where(pos < 2, fix2_ref[...], s2)
    else:
        prev2 = carry_ref[0:1, :]
        prev1 = carry_ref[1:2, :]
        s1 = jnp.where(row == 0, prev1, s1)
        s2 = jnp.where(row == 0, prev2, jnp.where(row == 1, prev1, s2))
    conv = wconv_ref[0:1, :] * s2 + wconv_ref[1:2, :] * s1 + wconv_ref[2:3, :] * cu
    y = _dot((bg * conv).astype(_BF16), wout_ref[...])
    out = x + _rms(y, g_ref[1:2, :])
    out_ref[...] = out.reshape(out_ref.shape)
    if sample:
        cu_ref[...] = cu
    else:
        tail = cu[tm - 2:tm, :]
        carry_ref[0:2, :] = tail
        state_ref[0] = tail


def _conv_prompt(h, g, w_in, w_conv, w_out):
    b, s, d = h.shape
    tm = ROW_TILE
    return pl.pallas_call(
        functools.partial(_conv_body, False),
        grid=(b, s // tm),
        in_specs=[
            pl.BlockSpec((1, tm, d), lambda i, t: (i, t, 0)),
            _const_spec(g.shape), _const_spec(w_in.shape), _const_spec(w_conv.shape),
            _const_spec(w_out.shape),
        ],
        out_specs=[
            pl.BlockSpec((1, tm, d), lambda i, t: (i, t, 0)),
            pl.BlockSpec((1, 2, d), lambda i, t: (i, 0, 0)),
        ],
        out_shape=[jax.ShapeDtypeStruct(h.shape, _F32), jax.ShapeDtypeStruct((b, 2, d), _F32)],
        scratch_shapes=[pltpu.VMEM((SUBLANES, d), _F32)],
        compiler_params=_params(("arbitrary", "arbitrary")),
        name="conv_prompt",
    )(h, g, w_in, w_conv, w_out)


def _conv_sample(h, fix1, fix2, g, w_in, w_conv, w_out):
    n, d = h.shape
    args = (h, fix1, fix2, g, w_in, w_conv, w_out)
    return pl.pallas_call(
        functools.partial(_conv_body, True),
        grid=(1,),
        in_specs=[_const_spec(a.shape) for a in args],
        out_specs=[_const_spec((n, d)), _const_spec((n, d))],
        out_shape=[jax.ShapeDtypeStruct((n, d), _F32), jax.ShapeDtypeStruct((n, d), _F32)],
        compiler_params=_params(("arbitrary",)),
        name="conv_sample",
    )(*args)


def _ffn_body(attn, *refs):
    if attn:
        (h_ref, o_ref, p_ref, g_ref, wo_ref, wg_ref, wu_ref, wd_ref, wple_ref, wpg_ref, out_ref) = refs
    else:
        (h_ref, p_ref, g_ref, wg_ref, wu_ref, wd_ref, wple_ref, wpg_ref, out_ref) = refs
    x = h_ref[...]
    if attn:
        x = x + _rms(_dot(o_ref[...], wo_ref[...]), g_ref[1:2, :])
    xn = _rms(x, g_ref[2:3, :]).astype(_BF16)
    gate = _dot(xn, wg_ref[...])
    up = _dot(xn, wu_ref[...])
    act = (gate * jax.nn.sigmoid(gate) * up).astype(_BF16)
    x = x + _rms(_dot(act, wd_ref[...]), g_ref[3:4, :])
    pg = jax.nn.sigmoid(_dot(x.astype(_BF16), wpg_ref[...]))
    pe = _dot(p_ref[...].astype(_BF16), wple_ref[...])
    out_ref[...] = x + pg * pe


def _ffn(h, o, p, g, w_o, w_gate, w_up, w_down, w_ple, w_ple_gate):
    n, d = h.shape
    tm = min(FFN_TILE, n)
    row = lambda c: pl.BlockSpec((tm, c), lambda t: (t, 0))
    attn = o is not None
    args = [h] + ([o] if attn else []) + [p, g] + ([w_o] if attn else []) + [
        w_gate, w_up, w_down, w_ple, w_ple_gate]
    specs = [row(d)] + ([row(d)] if attn else []) + [row(p.shape[1]), _const_spec(g.shape)] + (
        [_const_spec(w_o.shape)] if attn else []) + [
        _const_spec(w.shape) for w in (w_gate, w_up, w_down, w_ple, w_ple_gate)]
    return pl.pallas_call(
        functools.partial(_ffn_body, attn),
        grid=(n // tm,),
        in_specs=specs,
        out_specs=row(d),
        out_shape=jax.ShapeDtypeStruct((n, d), _F32),
        compiler_params=_params(("arbitrary",)),
        name="ffn_attn" if attn else "ffn",
    )(*args)


def _log_sigmoid(x):
    return jnp.minimum(x, 0.0) - jnp.log1p(jnp.exp(-jnp.abs(x)))


def _store_heads(ref, x):
    for hd in range(N_HEADS):
        ref[0, hd] = x[:, hd * HEAD_SLOT:(hd + 1) * HEAD_SLOT].astype(ref.dtype)


def _kvq_prompt_body(h_ref, gkv_ref, gq_ref, wk_ref, wv_ref, wvt_ref, wf_ref, bf_ref, wka_ref,
                     wqa_ref, tri_ref, scat_ref, ones_ref,
                     k_ref, v_ref, lf_ref, ka_ref, vt_ref, qa_ref, carry_ref):
    @pl.when(pl.program_id(1) == 0)
    def _():
        carry_ref[...] = jnp.zeros_like(carry_ref)

    x = h_ref[0]
    tm = x.shape[0]
    z = _rms(x, gkv_ref[...]).astype(_BF16)
    k_ref[0] = _dot(z, wk_ref[...])
    v_ref[0] = _dot(z, wv_ref[...])
    lf = _log_sigmoid(_dot(z, wf_ref[...]) + bf_ref[...])
    lane = lax.broadcasted_iota(jnp.int32, (1, LANES), 1)
    lf = jnp.where(lane < N_HEADS, lf, 0.0)
    lf_ref[0] = lf[:, :N_HEADS]
    hi, mid, lo = _split3(lf)
    tri = tri_ref[...]
    f = _dot(tri, hi) + _dot(tri, mid) + _dot(tri, lo) + carry_ref[0:1, :]
    carry_ref[0:1, :] = f[tm - 1:tm, :]
    nhi, nmid, nlo = _split3(-f)
    parts = (nhi.astype(_F32) + pltpu.roll(nmid.astype(_F32), N_HEADS, 1)
             + pltpu.roll(nlo.astype(_F32), 2 * N_HEADS, 1)).astype(_BF16)
    ka = _dot(z, wka_ref[...]) + _dot(parts, scat_ref[...])
    _store_heads(ka_ref, ka)
    vt = _dot_nt(wvt_ref[...], z)
    vt_ref[0, :, 0] = vt.astype(_BF16).reshape(N_HEADS, HEAD_DIM, tm)
    xq = _rms(x, gq_ref[...]).astype(_BF16)
    _store_heads(qa_ref, _dot(xq, wqa_ref[...]) + ones_ref[...])


def _q_prompt_body(h_ref, gq_ref, wqa_ref, ones_ref, qa_ref):
    xq = _rms(h_ref[0], gq_ref[...]).astype(_BF16)
    _store_heads(qa_ref, _dot(xq, wqa_ref[...]) + ones_ref[...])


def _head_major_spec(tm):
    return pl.BlockSpec((1, N_HEADS, tm, HEAD_SLOT), lambda i, t: (i, 0, t, 0))


def _kvq_prompt(h, gkv, gq, w_k, w_v, w_vt, w_f, b_f, w_ka, w_qa, tri, scat, ones):
    b, s, d = h.shape
    tm = ROW_TILE
    nk = s // tm
    consts = (gkv, gq, w_k, w_v, w_vt, w_f, b_f, w_ka, w_qa, tri, scat, ones)
    tile = lambda c: pl.BlockSpec((1, tm, c), lambda i, t: (i, t, 0))
    hm = jax.ShapeDtypeStruct((b, N_HEADS, s, HEAD_SLOT), _BF16)
    return pl.pallas_call(
        _kvq_prompt_body,
        grid=(b, nk),
        in_specs=[tile(d)] + [_const_spec(c.shape) for c in consts],
        out_specs=[tile(d), tile(d), tile(N_HEADS), _head_major_spec(tm),
                   pl.BlockSpec((1, N_HEADS, 1, HEAD_DIM, tm), lambda i, t: (i, 0, t, 0, 0)),
                   _head_major_spec(tm)],
        out_shape=[jax.ShapeDtypeStruct((b, s, d), _F32), jax.ShapeDtypeStruct((b, s, d), _F32),
                   jax.ShapeDtypeStruct((b, s, N_HEADS), _F32), hm,
                   jax.ShapeDtypeStruct((b, N_HEADS, nk, HEAD_DIM, tm), _BF16), hm],
        scratch_shapes=[pltpu.VMEM((SUBLANES, LANES), _F32)],
        compiler_params=_params(("arbitrary", "arbitrary")),
        name="kvq_prompt",
    )(h, *consts)


def _q_prompt(h, gq, w_qa, ones):
    b, s, d = h.shape
    tm = ROW_TILE
    consts = (gq, w_qa, ones)
    return pl.pallas_call(
        _q_prompt_body,
        grid=(b, s // tm),
        in_specs=[pl.BlockSpec((1, tm, d), lambda i, t: (i, t, 0))] + [_const_spec(c.shape) for c in consts],
        out_specs=_head_major_spec(tm),
        out_shape=jax.ShapeDtypeStruct((b, N_HEADS, s, HEAD_SLOT), _BF16),
        compiler_params=_params(("arbitrary", "arbitrary")),
        name="q_prompt",
    )(h, *consts)


def _kvq_sample_body(kv, *refs):
    if kv:
        (h_ref, gkv_ref, gq_ref, wk_ref, wv_ref, wf_ref, bf_ref, wq_ref,
         k_ref, v_ref, lf_ref, q_ref) = refs
        x = h_ref[...]
        z = _rms(x, gkv_ref[...]).astype(_BF16)
        k_ref[...] = _dot(z, wk_ref[...])
        v_ref[...] = _dot(z, wv_ref[...])
        lf_ref[...] = _log_sigmoid(_dot(z, wf_ref[...]) + bf_ref[...])[:, :N_HEADS]
    else:
        h_ref, gq_ref, wq_ref, q_ref = refs
        x = h_ref[...]
    q_ref[...] = _dot(_rms(x, gq_ref[...]).astype(_BF16), wq_ref[...])


def _kvq_sample(h, gkv, gq, w_k, w_v, w_f, b_f, w_q):
    n, d = h.shape
    args = (h, gkv, gq, w_k, w_v, w_f, b_f, w_q)
    outs = [(n, d), (n, d), (n, N_HEADS), (n, d)]
    return pl.pallas_call(
        functools.partial(_kvq_sample_body, True),
        grid=(1,),
        in_specs=[_const_spec(a.shape) for a in args],
        out_specs=[_const_spec(o) for o in outs],
        out_shape=[jax.ShapeDtypeStruct(o, _F32) for o in outs],
        compiler_params=_params(("arbitrary",)),
        name="kvq_sample",
    )(*args)


def _q_sample(h, gq, w_q):
    n, d = h.shape
    args = (h, gq, w_q)
    return pl.pallas_call(
        functools.partial(_kvq_sample_body, False),
        grid=(1,),
        in_specs=[_const_spec(a.shape) for a in args],
        out_specs=_const_spec((n, d)),
        out_shape=jax.ShapeDtypeStruct((n, d), _F32),
        compiler_params=_params(("arbitrary",)),
        name="q_sample",
    )(*args)


def _flash_body(q_ref, k_ref, vt_ref, o_ref):
    qi = pl.program_id(2)
    tq = q_ref.shape[2]
    tk = vt_ref.shape[4]
    qs = [q_ref[0, hd] for hd in range(HEAD_GROUP)]

    def scores(hd, j):
        kb = k_ref[0, hd, pl.ds(pl.multiple_of(j * tk, tk), tk), :]
        return _dot_nt(kb, qs[hd])

    key_pos = lax.broadcasted_iota(jnp.int32, (tk, tq), 0)
    qry_pos = lax.broadcasted_iota(jnp.int32, (tk, tq), 1)
    causal = key_pos <= qry_pos
    state = []
    for hd in range(HEAD_GROUP):
        st = jnp.where(causal, scores(hd, qi), -jnp.inf)
        m = jnp.max(st, axis=0, keepdims=True)
        p = jnp.exp(st - m)
        l = jnp.sum(p, axis=0, keepdims=True)
        acc = _dot(vt_ref[0, hd, qi], p.astype(_BF16))
        state += [m, l, acc]

    def step(j, carry):
        new = []
        for hd in range(HEAD_GROUP):
            m, l, acc = carry[3 * hd:3 * hd + 3]
            st = scores(hd, j)
            m_new = jnp.maximum(m, jnp.max(st, axis=0, keepdims=True))
            alpha = jnp.exp(m - m_new)
            p = jnp.exp(st - m_new)
            l = alpha * l + jnp.sum(p, axis=0, keepdims=True)
            acc = alpha * acc + _dot(vt_ref[0, hd, j], p.astype(_BF16))
            new += [m_new, l, acc]
        return tuple(new)

    state = lax.fori_loop(0, qi, step, tuple(state))
    outs = [state[3 * hd + 2] * (1.0 / state[3 * hd + 1]) for hd in range(HEAD_GROUP)]
    o_ref[0] = jnp.concatenate(outs, axis=0).T.astype(o_ref.dtype)


def _flash(qa, ka, vt):
    b, _, s, _ = qa.shape
    nk, tk = vt.shape[2], vt.shape[4]
    tq = Q_TILE
    assert tq == tk
    hg = HEAD_GROUP
    return pl.pallas_call(
        _flash_body,
        grid=(b, N_HEADS // hg, s // tq),
        in_specs=[
            pl.BlockSpec((1, hg, tq, HEAD_SLOT), lambda i, g, t: (i, g, t, 0)),
            pl.BlockSpec((1, hg, s, HEAD_SLOT), lambda i, g, t: (i, g, 0, 0)),
            pl.BlockSpec((1, hg, nk, HEAD_DIM, tk), lambda i, g, t: (i, g, 0, 0, 0)),
        ],
        out_specs=pl.BlockSpec((1, tq, hg * HEAD_DIM), lambda i, g, t: (i, t, g)),
        out_shape=jax.ShapeDtypeStruct((b, s, D_MODEL), _BF16),
        compiler_params=_params(("arbitrary", "arbitrary", "arbitrary")),
        name="flash_prompt",
    )(qa, ka, vt)


def _page_cumsum_body(x_ref, triu_ref, o_ref):
    x = x_ref[...].reshape(-1, PAGE)
    hi, mid, lo = _split3(x)
    tri = triu_ref[...]
    o_ref[...] = (_dot(hi, tri) + _dot(mid, tri) + _dot(lo, tri)).reshape(o_ref.shape)


def _page_cumsum(xt, triu, pages):
    n = xt.shape[0]
    spec = pl.BlockSpec((pages, N_HEADS, PAGE), lambda i: (i, 0, 0))
    return pl.pallas_call(
        _page_cumsum_body,
        grid=(n // pages,),
        in_specs=[spec, _const_spec(triu.shape)],
        out_specs=spec,
        out_shape=jax.ShapeDtypeStruct(xt.shape, _F32),
        compiler_params=_params(("arbitrary",)),
        name="page_cumsum",
    )(xt, triu)


def _expand_heads(f_ref):
    return jnp.concatenate(
        [jnp.broadcast_to(f_ref[0, hd:hd + 1, :], (SAMPLE_SEQ, PAGE)) for hd in range(N_HEADS)], axis=0)


def _paged_body(pt_ref, q_ref, *refs):
    npg = PAGES_PER_STEP
    k_refs = refs[:npg]
    v_refs = refs[npg:2 * npg]
    f_refs = refs[2 * npg:3 * npg]
    kn_ref, vn_ref, fn_ref, o_ref, qbd_ref, m_ref, l_ref, acc_ref, c_ref = refs[3 * npg:]
    j = pl.program_id(1)
    rows = N_HEADS * SAMPLE_SEQ
    head_of_row = lax.broadcasted_iota(jnp.int32, (rows, D_MODEL), 0) // SAMPLE_SEQ
    head_of_col = lax.broadcasted_iota(jnp.int32, (rows, D_MODEL), 1) // HEAD_DIM
    blockdiag = head_of_row == head_of_col

    @pl.when(j == 0)
    def _():
        q = q_ref[0]
        qbd_ref[...] = jnp.where(blockdiag, jnp.concatenate([q] * N_HEADS, axis=0), 0.0).astype(_BF16)
        m_ref[...] = jnp.full_like(m_ref, -jnp.inf)
        l_ref[...] = jnp.zeros_like(l_ref)
        acc_ref[...] = jnp.zeros_like(acc_ref)
        c_ref[...] = jnp.zeros_like(c_ref)

    qbd = qbd_ref[...]

    def attend(st_list, v_list):
        m = m_ref[...]
        m_new = m
        for st in st_list:
            m_new = jnp.maximum(m_new, jnp.max(st, axis=1, keepdims=True))
        alpha = jnp.exp(m - m_new)
        l = alpha * l_ref[...]
        pv = None
        for st, v in zip(st_list, v_list):
            p = jnp.exp(st - m_new)
            l = l + jnp.sum(p, axis=1, keepdims=True)
            d = _dot(p.astype(_BF16), v)
            pv = d if pv is None else pv + d
        m_ref[...] = m_new
        l_ref[...] = l
        acc_ref[...] = alpha * acc_ref[...] + pv

    c = c_ref[...]
    st_list, v_list = [], []
    for i in range(npg):
        floc = _expand_heads(f_refs[i])
        st = _dot_nt(qbd, k_refs[i][0].astype(_BF16)) - (floc + c)
        c = c + floc[:, PAGE - 1:PAGE]
        st_list.append(st)
        v_list.append(v_refs[i][0].astype(_BF16))
    attend(st_list, v_list)
    c_ref[...] = c

    @pl.when(j == pl.num_programs(1) - 1)
    def _():
        pad = jnp.zeros((PAGE - SAMPLE_SEQ, D_MODEL), _F32)
        kn = jnp.concatenate([kn_ref[0], pad], axis=0).astype(_BF16)
        vn = jnp.concatenate([vn_ref[0], pad], axis=0).astype(_BF16)
        st = _dot_nt(qbd, kn) - (_expand_heads(fn_ref) + c)
        key = lax.broadcasted_iota(jnp.int32, (rows, PAGE), 1)
        qpos = lax.broadcasted_iota(jnp.int32, (rows, PAGE), 0) % SAMPLE_SEQ
        attend([jnp.where(key <= qpos, st, -jnp.inf)], [vn])
        full = jnp.where(blockdiag, acc_ref[...] * (1.0 / l_ref[...]), 0.0)
        out = full[0:SAMPLE_SEQ, :]
        for hd in range(1, N_HEADS):
            out = out + full[hd * SAMPLE_SEQ:(hd + 1) * SAMPLE_SEQ, :]
        o_ref[0] = out.astype(o_ref.dtype)


def _paged(page_table, q, cache_k, cache_v, floc, k_new, v_new, floc_new):
    bd, t, d = q.shape
    assert t == SAMPLE_SEQ
    n_pages = page_table.shape[1]
    npg = PAGES_PER_STEP
    pt = page_table.reshape(-1)

    def page_spec(shape, i):
        return pl.BlockSpec(shape, lambda b, j, pt_ref: (pt_ref[b * n_pages + j * npg + i], 0, 0))

    per_b = lambda shape: pl.BlockSpec(shape, lambda b, j, pt_ref: (b, 0, 0))
    in_specs = ([per_b((1, t, d))]
                + [page_spec((1, PAGE, d), i) for i in range(npg)]
                + [page_spec((1, PAGE, d), i) for i in range(npg)]
                + [page_spec((1, N_HEADS, PAGE), i) for i in range(npg)]
                + [per_b((1, t, d)), per_b((1, t, d)), per_b((1, N_HEADS, PAGE))])
    rows = N_HEADS * t
    grid_spec = pltpu.PrefetchScalarGridSpec(
        num_scalar_prefetch=1,
        grid=(bd, n_pages // npg),
        in_specs=in_specs,
        out_specs=per_b((1, t, d)),
        scratch_shapes=[pltpu.VMEM((rows, d), _BF16), pltpu.VMEM((rows, 1), _F32),
                        pltpu.VMEM((rows, 1), _F32), pltpu.VMEM((rows, d), _F32),
                        pltpu.VMEM((rows, 1), _F32)],
    )
    return pl.pallas_call(
        _paged_body,
        grid_spec=grid_spec,
        out_shape=jax.ShapeDtypeStruct((bd, t, d), _BF16),
        compiler_params=_params(("arbitrary", "arbitrary")),
        name="paged_sample",
    )(pt, q, *([cache_k] * npg), *([cache_v] * npg), *([floc] * npg), k_new, v_new, floc_new)


def _head_slots(w):
    w = w.reshape(D_MODEL, N_HEADS, HEAD_DIM)
    w = jnp.pad(w, ((0, 0), (0, 0), (0, HEAD_SLOT - HEAD_DIM)))
    return w.reshape(D_MODEL, N_HEADS * HEAD_SLOT)


def _constants():
    tri = np.tril(np.ones((ROW_TILE, ROW_TILE), np.float32))
    triu = np.triu(np.ones((PAGE, PAGE), np.float32))
    scat = np.zeros((LANES, N_HEADS * HEAD_SLOT), np.float32)
    ones = np.zeros((1, N_HEADS * HEAD_SLOT), np.float32)
    for part in range(3):
        for hd in range(N_HEADS):
            scat[part * N_HEADS + hd, hd * HEAD_SLOT + BIAS_LANE + part] = 1.0
            ones[0, hd * HEAD_SLOT + BIAS_LANE + part] = 1.0
    return (jnp.asarray(tri, _BF16), jnp.asarray(triu, _BF16), jnp.asarray(scat, _BF16),
            jnp.asarray(ones, _F32))


def kernel(x_prompt, x_sample, state_conv, cache_k, cache_v, cache_logf, page_table, p_prompt, p_sample,
           norm_g, w_in, w_conv, w_out, g_kv, w_k, w_v, w_f, b_f, w_q, w_o,
           w_gate, w_up, w_down, w_ple, w_ple_gate):
    b, s, d = x_prompt.shape
    bd, t, _ = x_sample.shape
    n_pool = cache_k.shape[0]
    depth = norm_g.shape[0]
    bf = lambda w: w.astype(_BF16)
    tri, triu, scat, ones = _constants()

    w_in_b, w_out_b, w_o_b = bf(w_in), bf(w_out), bf(w_o)
    w_gate_b, w_up_b, w_down_b = bf(w_gate), bf(w_up), bf(w_down)
    w_ple_b, w_pg_b = bf(w_ple), bf(w_ple_gate)
    w_k_b, w_v_b, w_vt_b = bf(w_k), bf(w_v), bf(w_v.T)
    w_ka = bf(_head_slots(w_k))
    w_q_scaled = w_q * SCALE
    w_qa = [bf(_head_slots(w_q_scaled[j])) for j in range(w_q.shape[0])]
    w_qs = bf(w_q_scaled)
    w_f_pad = bf(jnp.pad(w_f, ((0, 0), (0, LANES - N_HEADS))))
    b_f_pad = jnp.pad(b_f, (0, LANES - N_HEADS)).reshape(1, LANES)
    gkv = g_kv.reshape(1, d)

    hp = x_prompt
    conv_prompt = []
    for i in range(N_CONV_LAYERS):
        hp, st = _conv_prompt(hp, norm_g[i], w_in_b[i], w_conv[i], w_out_b[i])
        conv_prompt.append(st)
        hp = _ffn(hp.reshape(b * s, d), None, p_prompt[i].reshape(b * s, PLE_DIM), norm_g[i], None,
                  w_gate_b[i], w_up_b[i], w_down_b[i], w_ple_b[i], w_pg_b[i]).reshape(b, s, d)
    k_p, v_p, lf_p, ka, vt, qa = _kvq_prompt(
        hp, gkv, norm_g[N_CONV_LAYERS, 0:1], w_k_b, w_v_b, w_vt_b, w_f_pad, b_f_pad, w_ka, w_qa[0],
        tri, scat, ones)
    for i in range(N_CONV_LAYERS, depth):
        jl = i - N_CONV_LAYERS
        if jl > 0:
            qa = _q_prompt(hp, norm_g[i, 0:1], w_qa[jl], ones)
        o = _flash(qa, ka, vt)
        hp = _ffn(hp.reshape(b * s, d), o.reshape(b * s, d), p_prompt[i].reshape(b * s, PLE_DIM),
                  norm_g[i], w_o_b[jl], w_gate_b[i], w_up_b[i], w_down_b[i], w_ple_b[i],
                  w_pg_b[i]).reshape(b, s, d)

    n = bd * t
    hs = x_sample.reshape(n, d)
    zeros_rows = jnp.zeros((bd, t - 2, d), _F32)
    conv_sample = []
    for i in range(N_CONV_LAYERS):
        st = state_conv[i]
        fix1 = jnp.concatenate([st[:, 1:2], st[:, 1:2], zeros_rows], axis=1).reshape(n, d)
        fix2 = jnp.concatenate([st, zeros_rows], axis=1).reshape(n, d)
        hs, cu = _conv_sample(hs, fix1, fix2, norm_g[i], w_in_b[i], w_conv[i], w_out_b[i])
        conv_sample.append(cu.reshape(bd, t, d)[:, t - 2:])
        hs = _ffn(hs, None, p_sample[i].reshape(n, PLE_DIM), norm_g[i], None,
                  w_gate_b[i], w_up_b[i], w_down_b[i], w_ple_b[i], w_pg_b[i])
    k_s, v_s, lf_s, q_s = _kvq_sample(hs, gkv, norm_g[N_CONV_LAYERS, 0:1], w_k_b, w_v_b, w_f_pad,
                                      b_f_pad, w_qs[0])
    floc = _page_cumsum(jnp.swapaxes(cache_logf, 1, 2), triu, PREPASS_PAGES)
    lf_new = jnp.pad(jnp.swapaxes(lf_s.reshape(bd, t, N_HEADS), 1, 2), ((0, 0), (0, 0), (0, PAGE - t)))
    floc_new = _page_cumsum(lf_new, triu, bd)
    ck = cache_k.reshape(n_pool, PAGE, d)
    cv = cache_v.reshape(n_pool, PAGE, d)
    k_new = k_s.reshape(bd, t, d)
    v_new = v_s.reshape(bd, t, d)
    for i in range(N_CONV_LAYERS, depth):
        jl = i - N_CONV_LAYERS
        if jl > 0:
            q_s = _q_sample(hs, norm_g[i, 0:1], w_qs[jl])
        o = _paged(page_table, q_s.reshape(bd, t, d), ck, cv, floc, k_new, v_new, floc_new)
        hs = _ffn(hs, o.reshape(n, d), p_sample[i].reshape(n, PLE_DIM), norm_g[i], w_o_b[jl],
                  w_gate_b[i], w_up_b[i], w_down_b[i], w_ple_b[i], w_pg_b[i])

    return (hp, hs.reshape(bd, t, d), jnp.stack(conv_prompt, axis=0),
            k_p.reshape(b, s, N_HEADS, HEAD_DIM), v_p.reshape(b, s, N_HEADS, HEAD_DIM), lf_p,
            jnp.stack(conv_sample, axis=0),
            k_s.reshape(bd, t, N_HEADS, HEAD_DIM), v_s.reshape(bd, t, N_HEADS, HEAD_DIM),
            lf_s.reshape(bd, t, N_HEADS))
```

```python
import functools

import jax
import jax.numpy as jnp
import numpy as np
from jax import lax
from jax.experimental import pallas as pl
from jax.experimental.pallas import tpu as pltpu

D_MODEL = 1024
N_HEADS = 16
HEAD_DIM = 64
D_FF = 2816
PLE_DIM = 256
EPS = 1e-6
N_CONV_LAYERS = 2
PAGE = 128
LANES = 128
SUBLANES = 8
SAMPLE_SEQ = 8
HEAD_SLOT = 128
BIAS_LANE = HEAD_DIM
SCALE = HEAD_DIM ** -0.5
LOG2E = 1.4426950408889634
VT_ROWS = HEAD_DIM + 16

ROW_TILE = 512
FFN_TILE = 256
Q_TILE = 512
HEAD_GROUP = 4
PAGES_PER_STEP = 4
PAGED_KEYS = 1024
PREPASS_PAGES = 128
VMEM_LIMIT = 56 * 1024 * 1024

_F32 = jnp.float32
_BF16 = jnp.bfloat16


def _rms(x, g):
    return x * lax.rsqrt(jnp.mean(x * x, axis=-1, keepdims=True) + EPS) * g


def _dot(a, b):
    return jnp.dot(a, b, preferred_element_type=_F32)


def _dot_nt(a, b):
    return lax.dot_general(a, b, (((1,), (1,)), ((), ())), preferred_element_type=_F32)


def _split3(x):
    hi = x.astype(_BF16)
    r = x - hi.astype(_F32)
    mid = r.astype(_BF16)
    lo = (r - mid.astype(_F32)).astype(_BF16)
    return hi, mid, lo


def _const_spec(shape):
    nd = len(shape)
    return pl.BlockSpec(shape, lambda *_: (0,) * nd, pipeline_mode=pl.Buffered(1))


def _params(sem):
    return pltpu.CompilerParams(dimension_semantics=sem, vmem_limit_bytes=VMEM_LIMIT)


def _conv_body(sample, *refs):
    if sample:
        (h_ref, fix1_ref, fix2_ref, g_ref, win_ref, wconv_ref, wout_ref, out_ref, cu_ref) = refs
    else:
        (h_ref, g_ref, win_ref, wconv_ref, wout_ref, out_ref, state_ref, carry_ref) = refs

        @pl.when(pl.program_id(1) == 0)
        def _():
            carry_ref[...] = jnp.zeros_like(carry_ref)

    x = h_ref[...].reshape(h_ref.shape[-2:])
    tm = x.shape[0]
    xn = _rms(x, g_ref[0:1, :]).astype(_BF16)
    proj = _dot(xn, win_ref[...])
    bg = proj[:, :D_MODEL]
    cu = proj[:, D_MODEL:2 * D_MODEL] * proj[:, 2 * D_MODEL:]
    s1 = pltpu.roll(cu, 1, 0)
    s2 = pltpu.roll(cu, 2, 0)
    row = lax.broadcasted_iota(jnp.int32, (tm, 1), 0)
    if sample:
        pos = row % SAMPLE_SEQ
        s1 = jnp.where(pos == 0, fix1_ref[...], s1)
        s2 = jnp.where(pos < 2, fix2_ref[...], s2)
    else:
        prev2 = carry_ref[0:1, :]
        prev1 = carry_ref[1:2, :]
        s1 = jnp.where(row == 0, prev1, s1)
        s2 = jnp.where(row == 0, prev2, jnp.where(row == 1, prev1, s2))
    conv = wconv_ref[0:1, :] * s2 + wconv_ref[1:2, :] * s1 + wconv_ref[2:3, :] * cu
    y = _dot((bg * conv).astype(_BF16), wout_ref[...])
    out = x + _rms(y, g_ref[1:2, :])
    out_ref[...] = out.reshape(out_ref.shape)
    if sample:
        cu_ref[...] = cu
    else:
        tail = cu[tm - 2:tm, :]
        carry_ref[0:2, :] = tail
        state_ref[0] = tail


def _conv_prompt(h, g, w_in, w_conv, w_out):
    b, s, d = h.shape
    tm = ROW_TILE
    return pl.pallas_call(
        functools.partial(_conv_body, False),
        grid=(b, s // tm),
        in_specs=[
            pl.BlockSpec((1, tm, d), lambda i, t: (i, t, 0)),
            _const_spec(g.shape), _const_spec(w_in.shape), _const_spec(w_conv.shape),
            _const_spec(w_out.shape),
        ],
        out_specs=[
            pl.BlockSpec((1, tm, d), lambda i, t: (i, t, 0)),
            pl.BlockSpec((1, 2, d), lambda i, t: (i, 0, 0)),
        ],
        out_shape=[jax.ShapeDtypeStruct(h.shape, _F32), jax.ShapeDtypeStruct((b, 2, d), _F32)],
        scratch_shapes=[pltpu.VMEM((SUBLANES, d), _F32)],
        compiler_params=_params(("arbitrary", "arbitrary")),
        name="conv_prompt",
    )(h, g, w_in, w_conv, w_out)


def _conv_sample(h, fix1, fix2, g, w_in, w_conv, w_out):
    n, d = h.shape
    args = (h, fix1, fix2, g, w_in, w_conv, w_out)
    return pl.pallas_call(
        functools.partial(_conv_body, True),
        grid=(1,),
        in_specs=[_const_spec(a.shape) for a in args],
        out_specs=[_const_spec((n, d)), _const_spec((n, d))],
        out_shape=[jax.ShapeDtypeStruct((n, d), _F32), jax.ShapeDtypeStruct((n, d), _F32)],
        compiler_params=_params(("arbitrary",)),
        name="conv_sample",
    )(*args)


def _ffn_body(attn, *refs):
    if attn:
        (h_ref, o_ref, p_ref, g_ref, wo_ref, wg_ref, wu_ref, wd_ref, wple_ref, wpg_ref, out_ref) = refs
    else:
        (h_ref, p_ref, g_ref, wg_ref, wu_ref, wd_ref, wple_ref, wpg_ref, out_ref) = refs
    x = h_ref[...]
    if attn:
        x = x + _rms(_dot(o_ref[...], wo_ref[...]), g_ref[1:2, :])
    xn = _rms(x, g_ref[2:3, :]).astype(_BF16)
    gate = _dot(xn, wg_ref[...])
    up = _dot(xn, wu_ref[...])
    act = (gate * jax.nn.sigmoid(gate) * up).astype(_BF16)
    x = x + _rms(_dot(act, wd_ref[...]), g_ref[3:4, :])
    pg = jax.nn.sigmoid(_dot(x.astype(_BF16), wpg_ref[...]))
    pe = _dot(p_ref[...].astype(_BF16), wple_ref[...])
    out_ref[...] = x + pg * pe


def _ffn(h, o, p, g, w_o, w_gate, w_up, w_down, w_ple, w_ple_gate):
    n, d = h.shape
    tm = min(FFN_TILE, n)
    row = lambda c: pl.BlockSpec((tm, c), lambda t: (t, 0))
    attn = o is not None
    args = [h] + ([o] if attn else []) + [p, g] + ([w_o] if attn else []) + [
        w_gate, w_up, w_down, w_ple, w_ple_gate]
    specs = [row(d)] + ([row(d)] if attn else []) + [row(p.shape[1]), _const_spec(g.shape)] + (
        [_const_spec(w_o.shape)] if attn else []) + [
        _const_spec(w.shape) for w in (w_gate, w_up, w_down, w_ple, w_ple_gate)]
    return pl.pallas_call(
        functools.partial(_ffn_body, attn),
        grid=(n // tm,),
        in_specs=specs,
        out_specs=row(d),
        out_shape=jax.ShapeDtypeStruct((n, d), _F32),
        compiler_params=_params(("arbitrary",)),
        name="ffn_attn" if attn else "ffn",
    )(*args)


def _log_sigmoid(x):
    return jnp.minimum(x, 0.0) - jnp.log1p(jnp.exp(-jnp.abs(x)))


def _store_heads(ref, x):
    for hd in range(N_HEADS):
        ref[0, hd] = x[:, hd * HEAD_SLOT:(hd + 1) * HEAD_SLOT].astype(ref.dtype)


def _kvq_prompt_body(h_ref, gkv_ref, gq_ref, wk_ref, wv_ref, wvt_ref, wf_ref, bf_ref, wka_ref,
                     wqa_ref, tri_ref, scat_ref, ones_ref,
                     k_ref, v_ref, lf_ref, ka_ref, vt_ref, qa_ref, carry_ref):
    @pl.when(pl.program_id(1) == 0)
    def _():
        carry_ref[...] = jnp.zeros_like(carry_ref)

    x = h_ref[0]
    tm = x.shape[0]
    z = _rms(x, gkv_ref[...]).astype(_BF16)
    k_ref[0] = _dot(z, wk_ref[...])
    v_ref[0] = _dot(z, wv_ref[...])
    lf = _log_sigmoid(_dot(z, wf_ref[...]) + bf_ref[...])
    lane = lax.broadcasted_iota(jnp.int32, (1, LANES), 1)
    lf = jnp.where(lane < N_HEADS, lf, 0.0)
    lf_ref[0] = lf[:, :N_HEADS]
    hi, mid, lo = _split3(lf)
    tri = tri_ref[...]
    f = _dot(tri, hi) + _dot(tri, mid) + _dot(tri, lo) + carry_ref[0:1, :]
    carry_ref[0:1, :] = f[tm - 1:tm, :]
    nhi, nmid, nlo = _split3(f * -LOG2E)
    parts = (nhi.astype(_F32) + pltpu.roll(nmid.astype(_F32), N_HEADS, 1)
             + pltpu.roll(nlo.astype(_F32), 2 * N_HEADS, 1)).astype(_BF16)
    ka = _dot(z, wka_ref[...]) + _dot(parts, scat_ref[...])
    _store_heads(ka_ref, ka)
    vt = _dot_nt(wvt_ref[...], z)
    vt_ref[0, :, 0, :HEAD_DIM, :] = vt.astype(_BF16).reshape(N_HEADS, HEAD_DIM, tm)
    extra = lax.broadcasted_iota(jnp.int32, (N_HEADS, VT_ROWS - HEAD_DIM, tm), 1) == 0
    vt_ref[0, :, 0, HEAD_DIM:, :] = extra.astype(_F32).astype(_BF16)
    xq = _rms(x, gq_ref[...]).astype(_BF16)
    _store_heads(qa_ref, _dot(xq, wqa_ref[...]) + ones_ref[...])


def _q_prompt_body(h_ref, gq_ref, wqa_ref, ones_ref, qa_ref):
    xq = _rms(h_ref[0], gq_ref[...]).astype(_BF16)
    _store_heads(qa_ref, _dot(xq, wqa_ref[...]) + ones_ref[...])


def _head_major_spec(tm):
    return pl.BlockSpec((1, N_HEADS, tm, HEAD_SLOT), lambda i, t: (i, 0, t, 0))


def _kvq_prompt(h, gkv, gq, w_k, w_v, w_vt, w_f, b_f, w_ka, w_qa, tri, scat, ones):
    b, s, d = h.shape
    tm = ROW_TILE
    nk = s // tm
    consts = (gkv, gq, w_k, w_v, w_vt, w_f, b_f, w_ka, w_qa, tri, scat, ones)
    tile = lambda c: pl.BlockSpec((1, tm, c), lambda i, t: (i, t, 0))
    hm = jax.ShapeDtypeStruct((b, N_HEADS, s, HEAD_SLOT), _BF16)
    return pl.pallas_call(
        _kvq_prompt_body,
        grid=(b, nk),
        in_specs=[tile(d)] + [_const_spec(c.shape) for c in consts],
        out_specs=[tile(d), tile(d), tile(N_HEADS), _head_major_spec(tm),
                   pl.BlockSpec((1, N_HEADS, 1, VT_ROWS, tm), lambda i, t: (i, 0, t, 0, 0)),
                   _head_major_spec(tm)],
        out_shape=[jax.ShapeDtypeStruct((b, s, d), _F32), jax.ShapeDtypeStruct((b, s, d), _F32),
                   jax.ShapeDtypeStruct((b, s, N_HEADS), _F32), hm,
                   jax.ShapeDtypeStruct((b, N_HEADS, nk, VT_ROWS, tm), _BF16), hm],
        scratch_shapes=[pltpu.VMEM((SUBLANES, LANES), _F32)],
        compiler_params=_params(("arbitrary", "arbitrary")),
        name="kvq_prompt",
    )(h, *consts)


def _q_prompt(h, gq, w_qa, ones):
    b, s, d = h.shape
    tm = ROW_TILE
    consts = (gq, w_qa, ones)
    return pl.pallas_call(
        _q_prompt_body,
        grid=(b, s // tm),
        in_specs=[pl.BlockSpec((1, tm, d), lambda i, t: (i, t, 0))] + [_const_spec(c.shape) for c in consts],
        out_specs=_head_major_spec(tm),
        out_shape=jax.ShapeDtypeStruct((b, N_HEADS, s, HEAD_SLOT), _BF16),
        compiler_params=_params(("arbitrary", "arbitrary")),
        name="q_prompt",
    )(h, *consts)


def _kvq_sample_body(kv, *refs):
    if kv:
        (h_ref, gkv_ref, gq_ref, wk_ref, wv_ref, wf_ref, bf_ref, wq_ref,
         k_ref, v_ref, lf_ref, q_ref) = refs
        x = h_ref[...]
        z = _rms(x, gkv_ref[...]).astype(_BF16)
        k_ref[...] = _dot(z, wk_ref[...])
        v_ref[...] = _dot(z, wv_ref[...])
        lf_ref[...] = _log_sigmoid(_dot(z, wf_ref[...]) + bf_ref[...])[:, :N_HEADS]
    else:
        h_ref, gq_ref, wq_ref, q_ref = refs
        x = h_ref[...]
    q_ref[...] = _dot(_rms(x, gq_ref[...]).astype(_BF16), wq_ref[...])


def _kvq_sample(h, gkv, gq, w_k, w_v, w_f, b_f, w_q):
    n, d = h.shape
    args = (h, gkv, gq, w_k, w_v, w_f, b_f, w_q)
    outs = [(n, d), (n, d), (n, N_HEADS), (n, d)]
    return pl.pallas_call(
        functools.partial(_kvq_sample_body, True),
        grid=(1,),
        in_specs=[_const_spec(a.shape) for a in args],
        out_specs=[_const_spec(o) for o in outs],
        out_shape=[jax.ShapeDtypeStruct(o, _F32) for o in outs],
        compiler_params=_params(("arbitrary",)),
        name="kvq_sample",
    )(*args)


def _q_sample(h, gq, w_q):
    n, d = h.shape
    args = (h, gq, w_q)
    return pl.pallas_call(
        functools.partial(_kvq_sample_body, False),
        grid=(1,),
        in_specs=[_const_spec(a.shape) for a in args],
        out_specs=_const_spec((n, d)),
        out_shape=jax.ShapeDtypeStruct((n, d), _F32),
        compiler_params=_params(("arbitrary",)),
        name="q_sample",
    )(*args)


def _flash_body(q_ref, k_ref, vt_ref, o_ref, s0_ref, s1_ref, m_ref, acc_ref):
    qi = pl.program_id(2)
    tq = q_ref.shape[2]
    tk = vt_ref.shape[4]
    slots = (s0_ref, s1_ref)

    def scores(j, slot):
        for hd in range(HEAD_GROUP):
            kb = k_ref[0, hd, pl.ds(pl.multiple_of(j * tk, tk), tk), :]
            slots[slot][hd] = _dot_nt(kb, q_ref[0, hd])

    def update(j, slot, masked):
        for hd in range(HEAD_GROUP):
            st = slots[slot][hd]
            if masked:
                key_pos = lax.broadcasted_iota(jnp.int32, (tk, tq), 0)
                qry_pos = lax.broadcasted_iota(jnp.int32, (tk, tq), 1)
                st = jnp.where(key_pos <= qry_pos, st, -jnp.inf)
            m = m_ref[hd]
            m_new = jnp.maximum(m, jnp.max(st, axis=0, keepdims=True))
            p = jnp.exp2(st - m_new).astype(_BF16)
            acc_ref[hd] = jnp.exp2(m - m_new) * acc_ref[hd] + _dot(vt_ref[0, hd, j], p)
            m_ref[hd] = m_new

    m_ref[...] = jnp.full_like(m_ref, -jnp.inf)
    acc_ref[...] = jnp.zeros_like(acc_ref)
    scores(0, 0)

    def pair(jj, carry):
        j = 2 * jj
        scores(j + 1, 1)
        update(j, 0, False)
        scores(j + 2, 0)
        update(j + 1, 1, False)
        return carry

    lax.fori_loop(0, qi // 2, pair, 0)

    @pl.when(qi % 2 == 0)
    def _():
        update(qi, 0, True)

    @pl.when(qi % 2 == 1)
    def _():
        scores(qi, 1)
        update(qi - 1, 0, False)
        update(qi, 1, True)

    outs = [acc_ref[hd, :HEAD_DIM, :] * (1.0 / acc_ref[hd, HEAD_DIM:HEAD_DIM + 1, :])
            for hd in range(HEAD_GROUP)]
    o_ref[0] = jnp.concatenate(outs, axis=0).T.astype(o_ref.dtype)


def _flash(qa, ka, vt):
    b, _, s, _ = qa.shape
    nk, tk = vt.shape[2], vt.shape[4]
    tq = Q_TILE
    assert tq == tk
    hg = HEAD_GROUP
    return pl.pallas_call(
        _flash_body,
        grid=(b, N_HEADS // hg, s // tq),
        in_specs=[
            pl.BlockSpec((1, hg, tq, HEAD_SLOT), lambda i, g, t: (i, g, t, 0)),
            pl.BlockSpec((1, hg, s, HEAD_SLOT), lambda i, g, t: (i, g, 0, 0)),
            pl.BlockSpec((1, hg, nk, VT_ROWS, tk), lambda i, g, t: (i, g, 0, 0, 0)),
        ],
        out_specs=pl.BlockSpec((1, tq, hg * HEAD_DIM), lambda i, g, t: (i, t, g)),
        out_shape=jax.ShapeDtypeStruct((b, s, D_MODEL), _BF16),
        scratch_shapes=[pltpu.VMEM((hg, tk, tq), _F32), pltpu.VMEM((hg, tk, tq), _F32),
                        pltpu.VMEM((hg, 1, tq), _F32), pltpu.VMEM((hg, VT_ROWS, tq), _F32)],
        compiler_params=_params(("arbitrary", "arbitrary", "arbitrary")),
        name="flash_prompt",
    )(qa, ka, vt)


def _page_cumsum_body(x_ref, triu_ref, o_ref):
    x = x_ref[...].reshape(-1, PAGE)
    hi, mid, lo = _split3(x)
    tri = triu_ref[...]
    o_ref[...] = (_dot(hi, tri) + _dot(mid, tri) + _dot(lo, tri)).reshape(o_ref.shape)


def _page_cumsum(xt, triu, pages):
    n = xt.shape[0]
    spec = pl.BlockSpec((pages, N_HEADS, PAGE), lambda i: (i, 0, 0))
    return pl.pallas_call(
        _page_cumsum_body,
        grid=(n // pages,),
        in_specs=[spec, _const_spec(triu.shape)],
        out_specs=spec,
        out_shape=jax.ShapeDtypeStruct(xt.shape, _F32),
        compiler_params=_params(("arbitrary",)),
        name="page_cumsum",
    )(xt, triu)


def _expand_heads(x):
    n = x.shape[1]
    return jnp.concatenate(
        [jnp.broadcast_to(x[hd:hd + 1, :], (SAMPLE_SEQ, n)) for hd in range(N_HEADS)], axis=0)


def _gather_body(pt_ref, *refs):
    npg = PAGES_PER_STEP
    k_refs = refs[:npg]
    v_refs = refs[npg:2 * npg]
    f_refs = refs[2 * npg:3 * npg]
    kc_ref, vc_ref, fo_ref, carry_ref = refs[3 * npg:]

    @pl.when(pl.program_id(1) == 0)
    def _():
        carry_ref[...] = jnp.zeros_like(carry_ref)

    c = carry_ref[...]
    for i in range(npg):
        rows = slice(i * PAGE, (i + 1) * PAGE)
        kc_ref[0, rows, :] = k_refs[i][0].astype(_BF16).reshape(PAGE, D_MODEL)
        vc_ref[0, rows, :] = v_refs[i][0].astype(_BF16).reshape(PAGE, D_MODEL)
        f = f_refs[i][0] + c
        fo_ref[0, :, rows] = f
        c = f[:, PAGE - 1:PAGE]
    carry_ref[...] = c


def _gather_pages(page_table, cache_k, cache_v, floc):
    bd, n_pages = page_table.shape
    npg = PAGES_PER_STEP
    past = n_pages * PAGE
    pt = page_table.reshape(-1)

    def page_spec(shape, i):
        nd = len(shape) - 1
        return pl.BlockSpec(shape, lambda b, j, pt_ref: (pt_ref[b * n_pages + j * npg + i],) + (0,) * nd)

    kv_block = (1, PAGE, N_HEADS, HEAD_DIM)
    in_specs = ([page_spec(kv_block, i) for i in range(npg)]
                + [page_spec(kv_block, i) for i in range(npg)]
                + [page_spec((1, N_HEADS, PAGE), i) for i in range(npg)])
    dense = pl.BlockSpec((1, npg * PAGE, D_MODEL), lambda b, j, pt_ref: (b, j, 0))
    grid_spec = pltpu.PrefetchScalarGridSpec(
        num_scalar_prefetch=1,
        grid=(bd, n_pages // npg),
        in_specs=in_specs,
        out_specs=[dense, dense, pl.BlockSpec((1, N_HEADS, npg * PAGE), lambda b, j, pt_ref: (b, 0, j))],
        scratch_shapes=[pltpu.VMEM((N_HEADS, 1), _F32)],
    )
    return pl.pallas_call(
        _gather_body,
        grid_spec=grid_spec,
        out_shape=[jax.ShapeDtypeStruct((bd, past, D_MODEL), _BF16),
                   jax.ShapeDtypeStruct((bd, past, D_MODEL), _BF16),
                   jax.ShapeDtypeStruct((bd, N_HEADS, past), _F32)],
        compiler_params=_params(("arbitrary", "arbitrary")),
        name="gather_pages",
    )(pt, *([cache_k] * npg), *([cache_v] * npg), *([floc] * npg))


def _paged_body(q_ref, k_ref, v_ref, f_ref, kn_ref, vn_ref, fn_ref, o_ref, qbd_ref, m_ref, l_ref, acc_ref):
    j = pl.program_id(1)
    rows = N_HEADS * SAMPLE_SEQ
    head_of_row = lax.broadcasted_iota(jnp.int32, (rows, D_MODEL), 0) // SAMPLE_SEQ
    head_of_col = lax.broadcasted_iota(jnp.int32, (rows, D_MODEL), 1) // HEAD_DIM
    blockdiag = head_of_row == head_of_col

    @pl.when(j == 0)
    def _():
        q = q_ref[0]
        qbd_ref[...] = jnp.where(blockdiag, jnp.concatenate([q] * N_HEADS, axis=0), 0.0).astype(_BF16)
        m_ref[...] = jnp.full_like(m_ref, -jnp.inf)
        l_ref[...] = jnp.zeros_like(l_ref)
        acc_ref[...] = jnp.zeros_like(acc_ref)

    qbd = qbd_ref[...]

    def attend(st, v):
        m = m_ref[...]
        m_new = jnp.maximum(m, jnp.max(st, axis=1, keepdims=True))
        alpha = jnp.exp(m - m_new)
        p = jnp.exp(st - m_new)
        l_ref[...] = alpha * l_ref[...] + jnp.sum(p, axis=1, keepdims=True)
        acc_ref[...] = alpha * acc_ref[...] + _dot(p.astype(_BF16), v)
        m_ref[...] = m_new

    f = f_ref[0]
    attend(_dot_nt(qbd, k_ref[0]) - _expand_heads(f), v_ref[0])

    @pl.when(j == pl.num_programs(1) - 1)
    def _():
        pad = jnp.zeros((PAGE - SAMPLE_SEQ, D_MODEL), _F32)
        kn = jnp.concatenate([kn_ref[0], pad], axis=0).astype(_BF16)
        vn = jnp.concatenate([vn_ref[0], pad], axis=0).astype(_BF16)
        f_new = fn_ref[0] + f[:, f.shape[1] - 1:]
        st = _dot_nt(qbd, kn) - _expand_heads(f_new)
        key = lax.broadcasted_iota(jnp.int32, (rows, PAGE), 1)
        qpos = lax.broadcasted_iota(jnp.int32, (rows, PAGE), 0) % SAMPLE_SEQ
        attend(jnp.where(key <= qpos, st, -jnp.inf), vn)
        full = jnp.where(blockdiag, acc_ref[...] * (1.0 / l_ref[...]), 0.0)
        out = full[0:SAMPLE_SEQ, :]
        for hd in range(1, N_HEADS):
            out = out + full[hd * SAMPLE_SEQ:(hd + 1) * SAMPLE_SEQ, :]
        o_ref[0] = out.astype(o_ref.dtype)


def _paged(q, kc, vc, f_past, k_new, v_new, floc_new):
    bd, t, d = q.shape
    assert t == SAMPLE_SEQ
    past = kc.shape[1]
    tk = PAGED_KEYS
    per_b = lambda shape: pl.BlockSpec(shape, lambda b, j: (b, 0, 0))
    in_specs = [per_b((1, t, d)),
                pl.BlockSpec((1, tk, d), lambda b, j: (b, j, 0)),
                pl.BlockSpec((1, tk, d), lambda b, j: (b, j, 0)),
                pl.BlockSpec((1, N_HEADS, tk), lambda b, j: (b, 0, j)),
                per_b((1, t, d)), per_b((1, t, d)), per_b((1, N_HEADS, PAGE))]
    rows = N_HEADS * t
    return pl.pallas_call(
        _paged_body,
        grid=(bd, past // tk),
        in_specs=in_specs,
        out_specs=per_b((1, t, d)),
        out_shape=jax.ShapeDtypeStruct((bd, t, d), _BF16),
        scratch_shapes=[pltpu.VMEM((rows, d), _BF16), pltpu.VMEM((rows, 1), _F32),
                        pltpu.VMEM((rows, 1), _F32), pltpu.VMEM((rows, d), _F32)],
        compiler_params=_params(("arbitrary", "arbitrary")),
        name="paged_sample",
    )(q, kc, vc, f_past, k_new, v_new, floc_new)


def _head_slots(w):
    w = w.reshape(D_MODEL, N_HEADS, HEAD_DIM)
    w = jnp.pad(w, ((0, 0), (0, 0), (0, HEAD_SLOT - HEAD_DIM)))
    return w.reshape(D_MODEL, N_HEADS * HEAD_SLOT)


def _constants():
    tri = np.tril(np.ones((ROW_TILE, ROW_TILE), np.float32))
    triu = np.triu(np.ones((PAGE, PAGE), np.float32))
    scat = np.zeros((LANES, N_HEADS * HEAD_SLOT), np.float32)
    ones = np.zeros((1, N_HEADS * HEAD_SLOT), np.float32)
    for part in range(3):
        for hd in range(N_HEADS):
            scat[part * N_HEADS + hd, hd * HEAD_SLOT + BIAS_LANE + part] = 1.0
            ones[0, hd * HEAD_SLOT + BIAS_LANE + part] = 1.0
    return (jnp.asarray(tri, _BF16), jnp.asarray(triu, _BF16), jnp.asarray(scat, _BF16),
            jnp.asarray(ones, _F32))


def kernel(x_prompt, x_sample, state_conv, cache_k, cache_v, cache_logf, page_table, p_prompt, p_sample,
           norm_g, w_in, w_conv, w_out, g_kv, w_k, w_v, w_f, b_f, w_q, w_o,
           w_gate, w_up, w_down, w_ple, w_ple_gate):
    b, s, d = x_prompt.shape
    bd, t, _ = x_sample.shape
    depth = norm_g.shape[0]
    bf = lambda w: w.astype(_BF16)
    tri, triu, scat, ones = _constants()

    w_in_b, w_out_b, w_o_b = bf(w_in), bf(w_out), bf(w_o)
    w_gate_b, w_up_b, w_down_b = bf(w_gate), bf(w_up), bf(w_down)
    w_ple_b, w_pg_b = bf(w_ple), bf(w_ple_gate)
    w_k_b, w_v_b, w_vt_b = bf(w_k), bf(w_v), bf(w_v.T)
    w_ka = bf(_head_slots(w_k))
    w_q_scaled = w_q * SCALE
    w_qa = [bf(_head_slots(w_q_scaled[j] * LOG2E)) for j in range(w_q.shape[0])]
    w_qs = bf(w_q_scaled)
    w_f_pad = bf(jnp.pad(w_f, ((0, 0), (0, LANES - N_HEADS))))
    b_f_pad = jnp.pad(b_f, (0, LANES - N_HEADS)).reshape(1, LANES)
    gkv = g_kv.reshape(1, d)

    hp = x_prompt
    conv_prompt = []
    for i in range(N_CONV_LAYERS):
        hp, st = _conv_prompt(hp, norm_g[i], w_in_b[i], w_conv[i], w_out_b[i])
        conv_prompt.append(st)
        hp = _ffn(hp.reshape(b * s, d), None, p_prompt[i].reshape(b * s, PLE_DIM), norm_g[i], None,
                  w_gate_b[i], w_up_b[i], w_down_b[i], w_ple_b[i], w_pg_b[i]).reshape(b, s, d)
    k_p, v_p, lf_p, ka, vt, qa = _kvq_prompt(
        hp, gkv, norm_g[N_CONV_LAYERS, 0:1], w_k_b, w_v_b, w_vt_b, w_f_pad, b_f_pad, w_ka, w_qa[0],
        tri, scat, ones)
    for i in range(N_CONV_LAYERS, depth):
        jl = i - N_CONV_LAYERS
        if jl > 0:
            qa = _q_prompt(hp, norm_g[i, 0:1], w_qa[jl], ones)
        o = _flash(qa, ka, vt)
        hp = _ffn(hp.reshape(b * s, d), o.reshape(b * s, d), p_prompt[i].reshape(b * s, PLE_DIM),
                  norm_g[i], w_o_b[jl], w_gate_b[i], w_up_b[i], w_down_b[i], w_ple_b[i],
                  w_pg_b[i]).reshape(b, s, d)

    n = bd * t
    hs = x_sample.reshape(n, d)
    zeros_rows = jnp.zeros((bd, t - 2, d), _F32)
    conv_sample = []
    for i in range(N_CONV_LAYERS):
        st = state_conv[i]
        fix1 = jnp.concatenate([st[:, 1:2], st[:, 1:2], zeros_rows], axis=1).reshape(n, d)
        fix2 = jnp.concatenate([st, zeros_rows], axis=1).reshape(n, d)
        hs, cu = _conv_sample(hs, fix1, fix2, norm_g[i], w_in_b[i], w_conv[i], w_out_b[i])
        conv_sample.append(cu.reshape(bd, t, d)[:, t - 2:])
        hs = _ffn(hs, None, p_sample[i].reshape(n, PLE_DIM), norm_g[i], None,
                  w_gate_b[i], w_up_b[i], w_down_b[i], w_ple_b[i], w_pg_b[i])
    k_s, v_s, lf_s, q_s = _kvq_sample(hs, gkv, norm_g[N_CONV_LAYERS, 0:1], w_k_b, w_v_b, w_f_pad,
                                      b_f_pad, w_qs[0])
    floc = _page_cumsum(jnp.swapaxes(cache_logf, 1, 2), triu, PREPASS_PAGES)
    lf_new = jnp.pad(jnp.swapaxes(lf_s.reshape(bd, t, N_HEADS), 1, 2), ((0, 0), (0, 0), (0, PAGE - t)))
    floc_new = _page_cumsum(lf_new, triu, bd)
    kc, vc, f_past = _gather_pages(page_table, cache_k, cache_v, floc)
    k_new = k_s.reshape(bd, t, d)
    v_new = v_s.reshape(bd, t, d)
    for i in range(N_CONV_LAYERS, depth):
        jl = i - N_CONV_LAYERS
        if jl > 0:
            q_s = _q_sample(hs, norm_g[i, 0:1], w_qs[jl])
        o = _paged(q_s.reshape(bd, t, d), kc, vc, f_past, k_new, v_new, floc_new)
        hs = _ffn(hs, o.reshape(n, d), p_sample[i].reshape(n, PLE_DIM), norm_g[i], w_o_b[jl],
                  w_gate_b[i], w_up_b[i], w_down_b[i], w_ple_b[i], w_pg_b[i])

    return (hp, hs.reshape(bd, t, d), jnp.stack(conv_prompt, axis=0),
            k_p.reshape(b, s, N_HEADS, HEAD_DIM), v_p.reshape(b, s, N_HEADS, HEAD_DIM), lf_p,
            jnp.stack(conv_sample, axis=0),
            k_s.reshape(bd, t, N_HEADS, HEAD_DIM), v_s.reshape(bd, t, N_HEADS, HEAD_DIM),
            lf_s.reshape(bd, t, N_HEADS))
```

```python
import functools

import jax
import jax.numpy as jnp
import numpy as np
from jax import lax
from jax.experimental import pallas as pl
from jax.experimental.pallas import tpu as pltpu

D_MODEL = 1024
N_HEADS = 16
HEAD_DIM = 64
D_FF = 2816
PLE_DIM = 256
EPS = 1e-6
N_CONV_LAYERS = 2
PAGE = 128
LANES = 128
SUBLANES = 8
SAMPLE_SEQ = 8
HEAD_SLOT = 128
BIAS_LANE = HEAD_DIM
SCALE = HEAD_DIM ** -0.5
LOG2E = 1.4426950408889634
VT_ROWS = HEAD_DIM + 16

ROW_TILE = 512
FFN_TILE = 256
Q_TILE = 512
HEAD_GROUP = 4
PAGES_PER_STEP = 4
PAGED_KEYS = 2048
PREPASS_PAGES = 128
VMEM_LIMIT = 56 * 1024 * 1024

_F32 = jnp.float32
_BF16 = jnp.bfloat16


def _rms(x, g):
    return x * lax.rsqrt(jnp.mean(x * x, axis=-1, keepdims=True) + EPS) * g


def _dot(a, b):
    return jnp.dot(a, b, preferred_element_type=_F32)


def _dot_nt(a, b):
    return lax.dot_general(a, b, (((1,), (1,)), ((), ())), preferred_element_type=_F32)


def _split3(x):
    hi = x.astype(_BF16)
    r = x - hi.astype(_F32)
    mid = r.astype(_BF16)
    lo = (r - mid.astype(_F32)).astype(_BF16)
    return hi, mid, lo


def _const_spec(shape):
    nd = len(shape)
    return pl.BlockSpec(shape, lambda *_: (0,) * nd, pipeline_mode=pl.Buffered(1))


def _params(sem):
    return pltpu.CompilerParams(dimension_semantics=sem, vmem_limit_bytes=VMEM_LIMIT)


def _conv_body(sample, *refs):
    if sample:
        (h_ref, fix1_ref, fix2_ref, g_ref, win_ref, wconv_ref, wout_ref, out_ref, cu_ref) = refs
    else:
        (h_ref, g_ref, win_ref, wconv_ref, wout_ref, out_ref, state_ref, carry_ref) = refs

        @pl.when(pl.program_id(1) == 0)
        def _():
            carry_ref[...] = jnp.zeros_like(carry_ref)

    x = h_ref[...].reshape(h_ref.shape[-2:])
    tm = x.shape[0]
    xn = _rms(x, g_ref[0:1, :]).astype(_BF16)
    proj = _dot(xn, win_ref[...])
    bg = proj[:, :D_MODEL]
    cu = proj[:, D_MODEL:2 * D_MODEL] * proj[:, 2 * D_MODEL:]
    s1 = pltpu.roll(cu, 1, 0)
    s2 = pltpu.roll(cu, 2, 0)
    row = lax.broadcasted_iota(jnp.int32, (tm, 1), 0)
    if sample:
        pos = row % SAMPLE_SEQ
        s1 = jnp.where(pos == 0, fix1_ref[...], s1)
        s2 = jnp.where(pos < 2, fix2_ref[...], s2)
    else:
        prev2 = carry_ref[0:1, :]
        prev1 = carry_ref[1:2, :]
        s1 = jnp.where(row == 0, prev1, s1)
        s2 = jnp.where(row == 0, prev2, jnp.where(row == 1, prev1, s2))
    conv = wconv_ref[0:1, :] * s2 + wconv_ref[1:2, :] * s1 + wconv_ref[2:3, :] * cu
    y = _dot((bg * conv).astype(_BF16), wout_ref[...])
    out = x + _rms(y, g_ref[1:2, :])
    out_ref[...] = out.reshape(out_ref.shape)
    if sample:
        cu_ref[...] = cu
    else:
        tail = cu[tm - 2:tm, :]
        carry_ref[0:2, :] = tail
        state_ref[0] = tail


def _conv_prompt(h, g, w_in, w_conv, w_out):
    b, s, d = h.shape
    tm = ROW_TILE
    return pl.pallas_call(
        functools.partial(_conv_body, False),
        grid=(b, s // tm),
        in_specs=[
            pl.BlockSpec((1, tm, d), lambda i, t: (i, t, 0)),
            _const_spec(g.shape), _const_spec(w_in.shape), _const_spec(w_conv.shape),
            _const_spec(w_out.shape),
        ],
        out_specs=[
            pl.BlockSpec((1, tm, d), lambda i, t: (i, t, 0)),
            pl.BlockSpec((1, 2, d), lambda i, t: (i, 0, 0)),
        ],
        out_shape=[jax.ShapeDtypeStruct(h.shape, _F32), jax.ShapeDtypeStruct((b, 2, d), _F32)],
        scratch_shapes=[pltpu.VMEM((SUBLANES, d), _F32)],
        compiler_params=_params(("arbitrary", "arbitrary")),
        name="conv_prompt",
    )(h, g, w_in, w_conv, w_out)


def _conv_sample(h, fix1, fix2, g, w_in, w_conv, w_out):
    n, d = h.shape
    args = (h, fix1, fix2, g, w_in, w_conv, w_out)
    return pl.pallas_call(
        functools.partial(_conv_body, True),
        grid=(1,),
        in_specs=[_const_spec(a.shape) for a in args],
        out_specs=[_const_spec((n, d)), _const_spec((n, d))],
        out_shape=[jax.ShapeDtypeStruct((n, d), _F32), jax.ShapeDtypeStruct((n, d), _F32)],
        compiler_params=_params(("arbitrary",)),
        name="conv_sample",
    )(*args)


def _ffn_body(attn, *refs):
    if attn:
        (h_ref, o_ref, p_ref, g_ref, wo_ref, wg_ref, wu_ref, wd_ref, wple_ref, wpg_ref, out_ref) = refs
    else:
        (h_ref, p_ref, g_ref, wg_ref, wu_ref, wd_ref, wple_ref, wpg_ref, out_ref) = refs
    x = h_ref[...]
    if attn:
        x = x + _rms(_dot(o_ref[...], wo_ref[...]), g_ref[1:2, :])
    xn = _rms(x, g_ref[2:3, :]).astype(_BF16)
    gate = _dot(xn, wg_ref[...])
    up = _dot(xn, wu_ref[...])
    act = (gate * jax.nn.sigmoid(gate) * up).astype(_BF16)
    x = x + _rms(_dot(act, wd_ref[...]), g_ref[3:4, :])
    pg = jax.nn.sigmoid(_dot(x.astype(_BF16), wpg_ref[...]))
    pe = _dot(p_ref[...].astype(_BF16), wple_ref[...])
    out_ref[...] = x + pg * pe


def _ffn(h, o, p, g, w_o, w_gate, w_up, w_down, w_ple, w_ple_gate):
    n, d = h.shape
    tm = min(FFN_TILE, n)
    row = lambda c: pl.BlockSpec((tm, c), lambda t: (t, 0))
    attn = o is not None
    args = [h] + ([o] if attn else []) + [p, g] + ([w_o] if attn else []) + [
        w_gate, w_up, w_down, w_ple, w_ple_gate]
    specs = [row(d)] + ([row(d)] if attn else []) + [row(p.shape[1]), _const_spec(g.shape)] + (
        [_const_spec(w_o.shape)] if attn else []) + [
        _const_spec(w.shape) for w in (w_gate, w_up, w_down, w_ple, w_ple_gate)]
    return pl.pallas_call(
        functools.partial(_ffn_body, attn),
        grid=(n // tm,),
        in_specs=specs,
        out_specs=row(d),
        out_shape=jax.ShapeDtypeStruct((n, d), _F32),
        compiler_params=_params(("arbitrary",)),
        name="ffn_attn" if attn else "ffn",
    )(*args)


def _log_sigmoid(x):
    return jnp.minimum(x, 0.0) - jnp.log1p(jnp.exp(-jnp.abs(x)))


def _store_heads(ref, x):
    for hd in range(N_HEADS):
        ref[0, hd] = x[:, hd * HEAD_SLOT:(hd + 1) * HEAD_SLOT].astype(ref.dtype)


def _kvq_prompt_body(h_ref, gkv_ref, gq_ref, wkt_ref, wvt_ref, wf_ref, bf_ref, wka_ref,
                     wqa_ref, tri_ref, scat_ref, ones_ref,
                     kt_ref, vto_ref, lf_ref, ka_ref, vt_ref, qa_ref, carry_ref):
    @pl.when(pl.program_id(1) == 0)
    def _():
        carry_ref[...] = jnp.zeros_like(carry_ref)

    x = h_ref[0]
    tm = x.shape[0]
    z = _rms(x, gkv_ref[...]).astype(_BF16)
    kt_ref[0] = _dot_nt(wkt_ref[...], z)
    vt = _dot_nt(wvt_ref[...], z)
    vto_ref[0] = vt
    lf = _log_sigmoid(_dot(z, wf_ref[...]) + bf_ref[...])
    lane = lax.broadcasted_iota(jnp.int32, (1, LANES), 1)
    lf = jnp.where(lane < N_HEADS, lf, 0.0)
    lf_ref[0] = lf[:, :N_HEADS]
    hi, mid, lo = _split3(lf)
    tri = tri_ref[...]
    f = _dot(tri, hi) + _dot(tri, mid) + _dot(tri, lo) + carry_ref[0:1, :]
    carry_ref[0:1, :] = f[tm - 1:tm, :]
    nhi, nmid, nlo = _split3(f * -LOG2E)
    parts = (nhi.astype(_F32) + pltpu.roll(nmid.astype(_F32), N_HEADS, 1)
             + pltpu.roll(nlo.astype(_F32), 2 * N_HEADS, 1)).astype(_BF16)
    ka = _dot(z, wka_ref[...]) + _dot(parts, scat_ref[...])
    _store_heads(ka_ref, ka)
    vt_ref[0, :, 0, :HEAD_DIM, :] = vt.astype(_BF16).reshape(N_HEADS, HEAD_DIM, tm)
    extra = lax.broadcasted_iota(jnp.int32, (N_HEADS, VT_ROWS - HEAD_DIM, tm), 1) == 0
    vt_ref[0, :, 0, HEAD_DIM:, :] = extra.astype(_F32).astype(_BF16)
    xq = _rms(x, gq_ref[...]).astype(_BF16)
    _store_heads(qa_ref, _dot(xq, wqa_ref[...]) + ones_ref[...])


def _q_prompt_body(h_ref, gq_ref, wqa_ref, ones_ref, qa_ref):
    xq = _rms(h_ref[0], gq_ref[...]).astype(_BF16)
    _store_heads(qa_ref, _dot(xq, wqa_ref[...]) + ones_ref[...])


def _head_major_spec(tm):
    return pl.BlockSpec((1, N_HEADS, tm, HEAD_SLOT), lambda i, t: (i, 0, t, 0))


def _kvq_prompt(h, gkv, gq, w_kt, w_vt, w_f, b_f, w_ka, w_qa, tri, scat, ones):
    b, s, d = h.shape
    tm = ROW_TILE
    nk = s // tm
    consts = (gkv, gq, w_kt, w_vt, w_f, b_f, w_ka, w_qa, tri, scat, ones)
    tile = lambda c: pl.BlockSpec((1, tm, c), lambda i, t: (i, t, 0))
    tile_t = pl.BlockSpec((1, d, tm), lambda i, t: (i, 0, t))
    hm = jax.ShapeDtypeStruct((b, N_HEADS, s, HEAD_SLOT), _BF16)
    return pl.pallas_call(
        _kvq_prompt_body,
        grid=(b, nk),
        in_specs=[tile(d)] + [_const_spec(c.shape) for c in consts],
        out_specs=[tile_t, tile_t, tile(N_HEADS), _head_major_spec(tm),
                   pl.BlockSpec((1, N_HEADS, 1, VT_ROWS, tm), lambda i, t: (i, 0, t, 0, 0)),
                   _head_major_spec(tm)],
        out_shape=[jax.ShapeDtypeStruct((b, d, s), _F32), jax.ShapeDtypeStruct((b, d, s), _F32),
                   jax.ShapeDtypeStruct((b, s, N_HEADS), _F32), hm,
                   jax.ShapeDtypeStruct((b, N_HEADS, nk, VT_ROWS, tm), _BF16), hm],
        scratch_shapes=[pltpu.VMEM((SUBLANES, LANES), _F32)],
        compiler_params=_params(("arbitrary", "arbitrary")),
        name="kvq_prompt",
    )(h, *consts)


def _q_prompt(h, gq, w_qa, ones):
    b, s, d = h.shape
    tm = ROW_TILE
    consts = (gq, w_qa, ones)
    return pl.pallas_call(
        _q_prompt_body,
        grid=(b, s // tm),
        in_specs=[pl.BlockSpec((1, tm, d), lambda i, t: (i, t, 0))] + [_const_spec(c.shape) for c in consts],
        out_specs=_head_major_spec(tm),
        out_shape=jax.ShapeDtypeStruct((b, N_HEADS, s, HEAD_SLOT), _BF16),
        compiler_params=_params(("arbitrary", "arbitrary")),
        name="q_prompt",
    )(h, *consts)


def _kvq_sample_body(kv, *refs):
    if kv:
        (h_ref, gkv_ref, gq_ref, wk_ref, wv_ref, wf_ref, bf_ref, wq_ref,
         k_ref, v_ref, lf_ref, q_ref) = refs
        x = h_ref[...]
        z = _rms(x, gkv_ref[...]).astype(_BF16)
        k_ref[...] = _dot(z, wk_ref[...])
        v_ref[...] = _dot(z, wv_ref[...])
        lf_ref[...] = _log_sigmoid(_dot(z, wf_ref[...]) + bf_ref[...])[:, :N_HEADS]
    else:
        h_ref, gq_ref, wq_ref, q_ref = refs
        x = h_ref[...]
    q_ref[...] = _dot(_rms(x, gq_ref[...]).astype(_BF16), wq_ref[...])


def _kvq_sample(h, gkv, gq, w_k, w_v, w_f, b_f, w_q):
    n, d = h.shape
    args = (h, gkv, gq, w_k, w_v, w_f, b_f, w_q)
    outs = [(n, d), (n, d), (n, N_HEADS), (n, d)]
    return pl.pallas_call(
        functools.partial(_kvq_sample_body, True),
        grid=(1,),
        in_specs=[_const_spec(a.shape) for a in args],
        out_specs=[_const_spec(o) for o in outs],
        out_shape=[jax.ShapeDtypeStruct(o, _F32) for o in outs],
        compiler_params=_params(("arbitrary",)),
        name="kvq_sample",
    )(*args)


def _q_sample(h, gq, w_q):
    n, d = h.shape
    args = (h, gq, w_q)
    return pl.pallas_call(
        functools.partial(_kvq_sample_body, False),
        grid=(1,),
        in_specs=[_const_spec(a.shape) for a in args],
        out_specs=_const_spec((n, d)),
        out_shape=jax.ShapeDtypeStruct((n, d), _F32),
        compiler_params=_params(("arbitrary",)),
        name="q_sample",
    )(*args)


def _flash_body(q_ref, k_ref, vt_ref, o_ref, s0_ref, s1_ref, m_ref, acc_ref):
    qi = pl.program_id(2)
    tq = q_ref.shape[2]
    tk = vt_ref.shape[4]
    slots = (s0_ref, s1_ref)

    def scores(j, slot):
        for hd in range(HEAD_GROUP):
            kb = k_ref[0, hd, pl.ds(pl.multiple_of(j * tk, tk), tk), :]
            slots[slot][hd] = _dot_nt(kb, q_ref[0, hd])

    def update(j, slot, masked):
        for hd in range(HEAD_GROUP):
            st = slots[slot][hd]
            if masked:
                key_pos = lax.broadcasted_iota(jnp.int32, (tk, tq), 0)
                qry_pos = lax.broadcasted_iota(jnp.int32, (tk, tq), 1)
                st = jnp.where(key_pos <= qry_pos, st, -jnp.inf)
            m = m_ref[hd]
            m_new = jnp.maximum(m, jnp.max(st, axis=0, keepdims=True))
            p = jnp.exp2(st - m_new).astype(_BF16)
            acc_ref[hd] = jnp.exp2(m - m_new) * acc_ref[hd] + _dot(vt_ref[0, hd, j], p)
            m_ref[hd] = m_new

    m_ref[...] = jnp.full_like(m_ref, -jnp.inf)
    acc_ref[...] = jnp.zeros_like(acc_ref)
    scores(0, 0)

    def pair(jj, carry):
        j = 2 * jj
        scores(j + 1, 1)
        update(j, 0, False)
        scores(j + 2, 0)
        update(j + 1, 1, False)
        return carry

    lax.fori_loop(0, qi // 2, pair, 0)

    @pl.when(qi % 2 == 0)
    def _():
        update(qi, 0, True)

    @pl.when(qi % 2 == 1)
    def _():
        scores(qi, 1)
        update(qi - 1, 0, False)
        update(qi, 1, True)

    outs = [acc_ref[hd, :HEAD_DIM, :] * (1.0 / acc_ref[hd, HEAD_DIM:HEAD_DIM + 1, :])
            for hd in range(HEAD_GROUP)]
    o_ref[0] = jnp.concatenate(outs, axis=0).T.astype(o_ref.dtype)


def _flash(qa, ka, vt):
    b, _, s, _ = qa.shape
    nk, tk = vt.shape[2], vt.shape[4]
    tq = Q_TILE
    assert tq == tk
    hg = HEAD_GROUP
    return pl.pallas_call(
        _flash_body,
        grid=(b, N_HEADS // hg, s // tq),
        in_specs=[
            pl.BlockSpec((1, hg, tq, HEAD_SLOT), lambda i, g, t: (i, g, t, 0)),
            pl.BlockSpec((1, hg, s, HEAD_SLOT), lambda i, g, t: (i, g, 0, 0)),
            pl.BlockSpec((1, hg, nk, VT_ROWS, tk), lambda i, g, t: (i, g, 0, 0, 0)),
        ],
        out_specs=pl.BlockSpec((1, tq, hg * HEAD_DIM), lambda i, g, t: (i, t, g)),
        out_shape=jax.ShapeDtypeStruct((b, s, D_MODEL), _BF16),
        scratch_shapes=[pltpu.VMEM((hg, tk, tq), _F32), pltpu.VMEM((hg, tk, tq), _F32),
                        pltpu.VMEM((hg, 1, tq), _F32), pltpu.VMEM((hg, VT_ROWS, tq), _F32)],
        compiler_params=_params(("arbitrary", "arbitrary", "arbitrary")),
        name="flash_prompt",
    )(qa, ka, vt)


def _page_cumsum_body(x_ref, triu_ref, o_ref):
    x = x_ref[...].reshape(-1, PAGE)
    hi, mid, lo = _split3(x)
    tri = triu_ref[...]
    o_ref[...] = (_dot(hi, tri) + _dot(mid, tri) + _dot(lo, tri)).reshape(o_ref.shape)


def _page_cumsum(xt, triu, pages):
    n = xt.shape[0]
    spec = pl.BlockSpec((pages, N_HEADS, PAGE), lambda i: (i, 0, 0))
    return pl.pallas_call(
        _page_cumsum_body,
        grid=(n // pages,),
        in_specs=[spec, _const_spec(triu.shape)],
        out_specs=spec,
        out_shape=jax.ShapeDtypeStruct(xt.shape, _F32),
        compiler_params=_params(("arbitrary",)),
        name="page_cumsum",
    )(xt, triu)


def _expand_heads(x):
    n = x.shape[1]
    return jnp.concatenate(
        [jnp.broadcast_to(x[hd:hd + 1, :], (SAMPLE_SEQ, n)) for hd in range(N_HEADS)], axis=0)


def _gather_body(pt_ref, *refs):
    npg = PAGES_PER_STEP
    k_refs = refs[:npg]
    v_refs = refs[npg:2 * npg]
    f_refs = refs[2 * npg:3 * npg]
    kc_ref, vc_ref, fo_ref, carry_ref = refs[3 * npg:]

    @pl.when(pl.program_id(1) == 0)
    def _():
        carry_ref[...] = jnp.zeros_like(carry_ref)

    c = carry_ref[...]
    for i in range(npg):
        keys = slice(i * PAGE, (i + 1) * PAGE)
        kc_ref[0, :, keys] = k_refs[i][0].reshape(D_MODEL, PAGE).astype(_BF16)
        vc_ref[0, :, keys] = v_refs[i][0].reshape(D_MODEL, PAGE).astype(_BF16)
        f = f_refs[i][0] + c
        fo_ref[0, :, keys] = f
        c = f[:, PAGE - 1:PAGE]
    carry_ref[...] = c


def _gather_pages(page_table, cache_kt, cache_vt, floc):
    bd, n_pages = page_table.shape
    npg = PAGES_PER_STEP
    past = n_pages * PAGE
    pt = page_table.reshape(-1)

    def page_spec(shape, i):
        nd = len(shape) - 1
        return pl.BlockSpec(shape, lambda b, j, pt_ref: (pt_ref[b * n_pages + j * npg + i],) + (0,) * nd)

    kv_block = (1, N_HEADS, HEAD_DIM, PAGE)
    in_specs = ([page_spec(kv_block, i) for i in range(npg)]
                + [page_spec(kv_block, i) for i in range(npg)]
                + [page_spec((1, N_HEADS, PAGE), i) for i in range(npg)])
    dense = pl.BlockSpec((1, D_MODEL, npg * PAGE), lambda b, j, pt_ref: (b, 0, j))
    grid_spec = pltpu.PrefetchScalarGridSpec(
        num_scalar_prefetch=1,
        grid=(bd, n_pages // npg),
        in_specs=in_specs,
        out_specs=[dense, dense, pl.BlockSpec((1, N_HEADS, npg * PAGE), lambda b, j, pt_ref: (b, 0, j))],
        scratch_shapes=[pltpu.VMEM((N_HEADS, 1), _F32)],
    )
    return pl.pallas_call(
        _gather_body,
        grid_spec=grid_spec,
        out_shape=[jax.ShapeDtypeStruct((bd, D_MODEL, past), _BF16),
                   jax.ShapeDtypeStruct((bd, D_MODEL, past), _BF16),
                   jax.ShapeDtypeStruct((bd, N_HEADS, past), _F32)],
        compiler_params=_params(("arbitrary", "arbitrary")),
        name="gather_pages",
    )(pt, *([cache_kt] * npg), *([cache_vt] * npg), *([floc] * npg))


def _paged_body(q_ref, k_ref, v_ref, f_ref, kn_ref, vn_ref, fn_ref, o_ref, qbd_ref, m_ref, l_ref, acc_ref):
    j = pl.program_id(1)
    rows = N_HEADS * SAMPLE_SEQ
    head_of_row = lax.broadcasted_iota(jnp.int32, (rows, D_MODEL), 0) // SAMPLE_SEQ
    head_of_col = lax.broadcasted_iota(jnp.int32, (rows, D_MODEL), 1) // HEAD_DIM
    blockdiag = head_of_row == head_of_col

    @pl.when(j == 0)
    def _():
        q = q_ref[0]
        qbd_ref[...] = jnp.where(blockdiag, jnp.concatenate([q] * N_HEADS, axis=0), 0.0).astype(_BF16)
        m_ref[...] = jnp.full_like(m_ref, -jnp.inf)
        l_ref[...] = jnp.zeros_like(l_ref)
        acc_ref[...] = jnp.zeros_like(acc_ref)

    qbd = qbd_ref[...]

    def attend(st, v, v_is_transposed):
        m = m_ref[...]
        m_new = jnp.maximum(m, jnp.max(st, axis=1, keepdims=True))
        alpha = jnp.exp(m - m_new)
        p = jnp.exp(st - m_new)
        l_ref[...] = alpha * l_ref[...] + jnp.sum(p, axis=1, keepdims=True)
        pv = _dot_nt(p.astype(_BF16), v) if v_is_transposed else _dot(p.astype(_BF16), v)
        acc_ref[...] = alpha * acc_ref[...] + pv
        m_ref[...] = m_new

    f = f_ref[0]
    attend(_dot(qbd, k_ref[0]) - _expand_heads(f), v_ref[0], True)

    @pl.when(j == pl.num_programs(1) - 1)
    def _():
        pad = jnp.zeros((PAGE - SAMPLE_SEQ, D_MODEL), _F32)
        kn = jnp.concatenate([kn_ref[0], pad], axis=0).astype(_BF16)
        vn = jnp.concatenate([vn_ref[0], pad], axis=0).astype(_BF16)
        f_new = fn_ref[0] + f[:, f.shape[1] - 1:]
        st = _dot_nt(qbd, kn) - _expand_heads(f_new)
        key = lax.broadcasted_iota(jnp.int32, (rows, PAGE), 1)
        qpos = lax.broadcasted_iota(jnp.int32, (rows, PAGE), 0) % SAMPLE_SEQ
        attend(jnp.where(key <= qpos, st, -jnp.inf), vn, False)
        full = jnp.where(blockdiag, acc_ref[...] * (1.0 / l_ref[...]), 0.0)
        out = full[0:SAMPLE_SEQ, :]
        for hd in range(1, N_HEADS):
            out = out + full[hd * SAMPLE_SEQ:(hd + 1) * SAMPLE_SEQ, :]
        o_ref[0] = out.astype(o_ref.dtype)


def _paged(q, kc, vc, f_past, k_new, v_new, floc_new):
    bd, t, d = q.shape
    assert t == SAMPLE_SEQ
    past = kc.shape[2]
    tk = PAGED_KEYS
    per_b = lambda shape: pl.BlockSpec(shape, lambda b, j: (b, 0, 0))
    in_specs = [per_b((1, t, d)),
                pl.BlockSpec((1, d, tk), lambda b, j: (b, 0, j)),
                pl.BlockSpec((1, d, tk), lambda b, j: (b, 0, j)),
                pl.BlockSpec((1, N_HEADS, tk), lambda b, j: (b, 0, j)),
                per_b((1, t, d)), per_b((1, t, d)), per_b((1, N_HEADS, PAGE))]
    rows = N_HEADS * t
    return pl.pallas_call(
        _paged_body,
        grid=(bd, past // tk),
        in_specs=in_specs,
        out_specs=per_b((1, t, d)),
        out_shape=jax.ShapeDtypeStruct((bd, t, d), _BF16),
        scratch_shapes=[pltpu.VMEM((rows, d), _BF16), pltpu.VMEM((rows, 1), _F32),
                        pltpu.VMEM((rows, 1), _F32), pltpu.VMEM((rows, d), _F32)],
        compiler_params=_params(("arbitrary", "arbitrary")),
        name="paged_sample",
    )(q, kc, vc, f_past, k_new, v_new, floc_new)


def _head_slots(w):
    w = w.reshape(D_MODEL, N_HEADS, HEAD_DIM)
    w = jnp.pad(w, ((0, 0), (0, 0), (0, HEAD_SLOT - HEAD_DIM)))
    return w.reshape(D_MODEL, N_HEADS * HEAD_SLOT)


def _constants():
    tri = np.tril(np.ones((ROW_TILE, ROW_TILE), np.float32))
    triu = np.triu(np.ones((PAGE, PAGE), np.float32))
    scat = np.zeros((LANES, N_HEADS * HEAD_SLOT), np.float32)
    ones = np.zeros((1, N_HEADS * HEAD_SLOT), np.float32)
    for part in range(3):
        for hd in range(N_HEADS):
            scat[part * N_HEADS + hd, hd * HEAD_SLOT + BIAS_LANE + part] = 1.0
            ones[0, hd * HEAD_SLOT + BIAS_LANE + part] = 1.0
    return (jnp.asarray(tri, _BF16), jnp.asarray(triu, _BF16), jnp.asarray(scat, _BF16),
            jnp.asarray(ones, _F32))


def kernel(x_prompt, x_sample, state_conv, cache_k, cache_v, cache_logf, page_table, p_prompt, p_sample,
           norm_g, w_in, w_conv, w_out, g_kv, w_k, w_v, w_f, b_f, w_q, w_o,
           w_gate, w_up, w_down, w_ple, w_ple_gate):
    b, s, d = x_prompt.shape
    bd, t, _ = x_sample.shape
    depth = norm_g.shape[0]
    bf = lambda w: w.astype(_BF16)
    tri, triu, scat, ones = _constants()

    w_in_b, w_out_b, w_o_b = bf(w_in), bf(w_out), bf(w_o)
    w_gate_b, w_up_b, w_down_b = bf(w_gate), bf(w_up), bf(w_down)
    w_ple_b, w_pg_b = bf(w_ple), bf(w_ple_gate)
    w_k_b, w_v_b, w_kt_b, w_vt_b = bf(w_k), bf(w_v), bf(w_k.T), bf(w_v.T)
    w_ka = bf(_head_slots(w_k))
    w_q_scaled = w_q * SCALE
    w_qa = [bf(_head_slots(w_q_scaled[j] * LOG2E)) for j in range(w_q.shape[0])]
    w_qs = bf(w_q_scaled)
    w_f_pad = bf(jnp.pad(w_f, ((0, 0), (0, LANES - N_HEADS))))
    b_f_pad = jnp.pad(b_f, (0, LANES - N_HEADS)).reshape(1, LANES)
    gkv = g_kv.reshape(1, d)

    hp = x_prompt
    conv_prompt = []
    for i in range(N_CONV_LAYERS):
        hp, st = _conv_prompt(hp, norm_g[i], w_in_b[i], w_conv[i], w_out_b[i])
        conv_prompt.append(st)
        hp = _ffn(hp.reshape(b * s, d), None, p_prompt[i].reshape(b * s, PLE_DIM), norm_g[i], None,
                  w_gate_b[i], w_up_b[i], w_down_b[i], w_ple_b[i], w_pg_b[i]).reshape(b, s, d)
    kt_p, vt_p, lf_p, ka, vt, qa = _kvq_prompt(
        hp, gkv, norm_g[N_CONV_LAYERS, 0:1], w_kt_b, w_vt_b, w_f_pad, b_f_pad, w_ka, w_qa[0],
        tri, scat, ones)
    heads_last = lambda xt: jnp.transpose(xt.reshape(b, N_HEADS, HEAD_DIM, s), (0, 3, 1, 2))
    for i in range(N_CONV_LAYERS, depth):
        jl = i - N_CONV_LAYERS
        if jl > 0:
            qa = _q_prompt(hp, norm_g[i, 0:1], w_qa[jl], ones)
        o = _flash(qa, ka, vt)
        hp = _ffn(hp.reshape(b * s, d), o.reshape(b * s, d), p_prompt[i].reshape(b * s, PLE_DIM),
                  norm_g[i], w_o_b[jl], w_gate_b[i], w_up_b[i], w_down_b[i], w_ple_b[i],
                  w_pg_b[i]).reshape(b, s, d)

    n = bd * t
    hs = x_sample.reshape(n, d)
    zeros_rows = jnp.zeros((bd, t - 2, d), _F32)
    conv_sample = []
    for i in range(N_CONV_LAYERS):
        st = state_conv[i]
        fix1 = jnp.concatenate([st[:, 1:2], st[:, 1:2], zeros_rows], axis=1).reshape(n, d)
        fix2 = jnp.concatenate([st, zeros_rows], axis=1).reshape(n, d)
        hs, cu = _conv_sample(hs, fix1, fix2, norm_g[i], w_in_b[i], w_conv[i], w_out_b[i])
        conv_sample.append(cu.reshape(bd, t, d)[:, t - 2:])
        hs = _ffn(hs, None, p_sample[i].reshape(n, PLE_DIM), norm_g[i], None,
                  w_gate_b[i], w_up_b[i], w_down_b[i], w_ple_b[i], w_pg_b[i])
    k_s, v_s, lf_s, q_s = _kvq_sample(hs, gkv, norm_g[N_CONV_LAYERS, 0:1], w_k_b, w_v_b, w_f_pad,
                                      b_f_pad, w_qs[0])
    floc = _page_cumsum(jnp.swapaxes(cache_logf, 1, 2), triu, PREPASS_PAGES)
    lf_new = jnp.pad(jnp.swapaxes(lf_s.reshape(bd, t, N_HEADS), 1, 2), ((0, 0), (0, 0), (0, PAGE - t)))
    floc_new = _page_cumsum(lf_new, triu, bd)
    to_key_minor = lambda c: jnp.transpose(c, (0, 2, 3, 1))
    kc, vc, f_past = _gather_pages(page_table, to_key_minor(cache_k), to_key_minor(cache_v), floc)
    k_new = k_s.reshape(bd, t, d)
    v_new = v_s.reshape(bd, t, d)
    for i in range(N_CONV_LAYERS, depth):
        jl = i - N_CONV_LAYERS
        if jl > 0:
            q_s = _q_sample(hs, norm_g[i, 0:1], w_qs[jl])
        o = _paged(q_s.reshape(bd, t, d), kc, vc, f_past, k_new, v_new, floc_new)
        hs = _ffn(hs, o.reshape(n, d), p_sample[i].reshape(n, PLE_DIM), norm_g[i], w_o_b[jl],
                  w_gate_b[i], w_up_b[i], w_down_b[i], w_ple_b[i], w_pg_b[i])

    return (hp, hs.reshape(bd, t, d), jnp.stack(conv_prompt, axis=0),
            heads_last(kt_p), heads_last(vt_p), lf_p,
            jnp.stack(conv_sample, axis=0),
            k_s.reshape(bd, t, N_HEADS, HEAD_DIM), v_s.reshape(bd, t, N_HEADS, HEAD_DIM),
            lf_s.reshape(bd, t, N_HEADS))
```

```python
import functools

import jax
import jax.numpy as jnp
import numpy as np
from jax import lax
from jax.experimental import pallas as pl
from jax.experimental.pallas import tpu as pltpu

D_MODEL = 1024
N_HEADS = 16
HEAD_DIM = 64
D_FF = 2816
PLE_DIM = 256
EPS = 1e-6
N_CONV_LAYERS = 2
PAGE = 128
LANES = 128
SUBLANES = 8
SAMPLE_SEQ = 8
HEAD_SLOT = 128
BIAS_LANE = HEAD_DIM
SCALE = HEAD_DIM ** -0.5
LOG2E = 1.4426950408889634
VT_ROWS = HEAD_DIM + 16

ROW_TILE = 512
FFN_TILE = 256
Q_TILE = 512
FLASH_Q_SPLIT = 256
HEAD_GROUP = 4
PAGES_PER_STEP = 8
PAGED_KEYS = 2048
PREPASS_PAGES = 128
VMEM_LIMIT = 56 * 1024 * 1024

_F32 = jnp.float32
_BF16 = jnp.bfloat16


def _rms(x, g):
    return x * lax.rsqrt(jnp.mean(x * x, axis=-1, keepdims=True) + EPS) * g


def _dot(a, b):
    return jnp.dot(a, b, preferred_element_type=_F32)


def _dot_nt(a, b):
    return lax.dot_general(a, b, (((1,), (1,)), ((), ())), preferred_element_type=_F32)


def _split3(x):
    hi = x.astype(_BF16)
    r = x - hi.astype(_F32)
    mid = r.astype(_BF16)
    lo = (r - mid.astype(_F32)).astype(_BF16)
    return hi, mid, lo


def _const_spec(shape):
    nd = len(shape)
    return pl.BlockSpec(shape, lambda *_: (0,) * nd, pipeline_mode=pl.Buffered(1))


def _params(sem):
    return pltpu.CompilerParams(dimension_semantics=sem, vmem_limit_bytes=VMEM_LIMIT)


def _conv_body(sample, *refs):
    if sample:
        (h_ref, fix1_ref, fix2_ref, g_ref, win_ref, wconv_ref, wout_ref, out_ref, cu_ref) = refs
    else:
        (h_ref, g_ref, win_ref, wconv_ref, wout_ref, out_ref, state_ref, carry_ref) = refs

        @pl.when(pl.program_id(1) == 0)
        def _():
            carry_ref[...] = jnp.zeros_like(carry_ref)

    x = h_ref[...].reshape(h_ref.shape[-2:])
    tm = x.shape[0]
    xn = _rms(x, g_ref[0:1, :]).astype(_BF16)
    proj = _dot(xn, win_ref[...])
    bg = proj[:, :D_MODEL]
    cu = proj[:, D_MODEL:2 * D_MODEL] * proj[:, 2 * D_MODEL:]
    s1 = pltpu.roll(cu, 1, 0)
    s2 = pltpu.roll(cu, 2, 0)
    row = lax.broadcasted_iota(jnp.int32, (tm, 1), 0)
    if sample:
        pos = row % SAMPLE_SEQ
        s1 = jnp.where(pos == 0, fix1_ref[...], s1)
        s2 = jnp.where(pos < 2, fix2_ref[...], s2)
    else:
        prev2 = carry_ref[0:1, :]
        prev1 = carry_ref[1:2, :]
        s1 = jnp.where(row == 0, prev1, s1)
        s2 = jnp.where(row == 0, prev2, jnp.where(row == 1, prev1, s2))
    conv = wconv_ref[0:1, :] * s2 + wconv_ref[1:2, :] * s1 + wconv_ref[2:3, :] * cu
    y = _dot((bg * conv).astype(_BF16), wout_ref[...])
    out = x + _rms(y, g_ref[1:2, :])
    out_ref[...] = out.reshape(out_ref.shape)
    if sample:
        cu_ref[...] = cu
    else:
        tail = cu[tm - 2:tm, :]
        carry_ref[0:2, :] = tail
        state_ref[0] = tail


def _conv_prompt(h, g, w_in, w_conv, w_out):
    b, s, d = h.shape
    tm = ROW_TILE
    return pl.pallas_call(
        functools.partial(_conv_body, False),
        grid=(b, s // tm),
        in_specs=[
            pl.BlockSpec((1, tm, d), lambda i, t: (i, t, 0)),
            _const_spec(g.shape), _const_spec(w_in.shape), _const_spec(w_conv.shape),
            _const_spec(w_out.shape),
        ],
        out_specs=[
            pl.BlockSpec((1, tm, d), lambda i, t: (i, t, 0)),
            pl.BlockSpec((1, 2, d), lambda i, t: (i, 0, 0)),
        ],
        out_shape=[jax.ShapeDtypeStruct(h.shape, _F32), jax.ShapeDtypeStruct((b, 2, d), _F32)],
        scratch_shapes=[pltpu.VMEM((SUBLANES, d), _F32)],
        compiler_params=_params(("arbitrary", "arbitrary")),
        name="conv_prompt",
    )(h, g, w_in, w_conv, w_out)


def _conv_sample(h, fix1, fix2, g, w_in, w_conv, w_out):
    n, d = h.shape
    args = (h, fix1, fix2, g, w_in, w_conv, w_out)
    return pl.pallas_call(
        functools.partial(_conv_body, True),
        grid=(1,),
        in_specs=[_const_spec(a.shape) for a in args],
        out_specs=[_const_spec((n, d)), _const_spec((n, d))],
        out_shape=[jax.ShapeDtypeStruct((n, d), _F32), jax.ShapeDtypeStruct((n, d), _F32)],
        compiler_params=_params(("arbitrary",)),
        name="conv_sample",
    )(*args)


def _ffn_body(attn, *refs):
    if attn:
        (h_ref, o_ref, p_ref, g_ref, wo_ref, wg_ref, wu_ref, wd_ref, wple_ref, wpg_ref, out_ref) = refs
    else:
        (h_ref, p_ref, g_ref, wg_ref, wu_ref, wd_ref, wple_ref, wpg_ref, out_ref) = refs
    x = h_ref[...]
    if attn:
        x = x + _rms(_dot(o_ref[...], wo_ref[...]), g_ref[1:2, :])
    xn = _rms(x, g_ref[2:3, :]).astype(_BF16)
    gate = _dot(xn, wg_ref[...])
    up = _dot(xn, wu_ref[...])
    act = (gate * jax.nn.sigmoid(gate) * up).astype(_BF16)
    x = x + _rms(_dot(act, wd_ref[...]), g_ref[3:4, :])
    pg = jax.nn.sigmoid(_dot(x.astype(_BF16), wpg_ref[...]))
    pe = _dot(p_ref[...].astype(_BF16), wple_ref[...])
    out_ref[...] = x + pg * pe


def _ffn(h, o, p, g, w_o, w_gate, w_up, w_down, w_ple, w_ple_gate):
    n, d = h.shape
    tm = min(FFN_TILE, n)
    row = lambda c: pl.BlockSpec((tm, c), lambda t: (t, 0))
    attn = o is not None
    args = [h] + ([o] if attn else []) + [p, g] + ([w_o] if attn else []) + [
        w_gate, w_up, w_down, w_ple, w_ple_gate]
    specs = [row(d)] + ([row(d)] if attn else []) + [row(p.shape[1]), _const_spec(g.shape)] + (
        [_const_spec(w_o.shape)] if attn else []) + [
        _const_spec(w.shape) for w in (w_gate, w_up, w_down, w_ple, w_ple_gate)]
    return pl.pallas_call(
        functools.partial(_ffn_body, attn),
        grid=(n // tm,),
        in_specs=specs,
        out_specs=row(d),
        out_shape=jax.ShapeDtypeStruct((n, d), _F32),
        compiler_params=_params(("arbitrary",)),
        name="ffn_attn" if attn else "ffn",
    )(*args)


def _log_sigmoid(x):
    return jnp.minimum(x, 0.0) - jnp.log1p(jnp.exp(-jnp.abs(x)))


def _store_heads(ref, x):
    for hd in range(N_HEADS):
        ref[0, hd] = x[:, hd * HEAD_SLOT:(hd + 1) * HEAD_SLOT].astype(ref.dtype)


def _kvq_prompt_body(h_ref, gkv_ref, gq_ref, wkt_ref, wvt_ref, wf_ref, bf_ref, wka_ref,
                     wqa_ref, tri_ref, scat_ref, ones_ref,
                     kt_ref, vto_ref, lf_ref, ka_ref, vt_ref, qa_ref, carry_ref):
    @pl.when(pl.program_id(1) == 0)
    def _():
        carry_ref[...] = jnp.zeros_like(carry_ref)

    x = h_ref[0]
    tm = x.shape[0]
    z = _rms(x, gkv_ref[...]).astype(_BF16)
    kt_ref[0] = _dot_nt(wkt_ref[...], z)
    vt = _dot_nt(wvt_ref[...], z)
    vto_ref[0] = vt
    lf = _log_sigmoid(_dot(z, wf_ref[...]) + bf_ref[...])
    lane = lax.broadcasted_iota(jnp.int32, (1, LANES), 1)
    lf = jnp.where(lane < N_HEADS, lf, 0.0)
    lf_ref[0] = lf[:, :N_HEADS]
    hi, mid, lo = _split3(lf)
    tri = tri_ref[...]
    f = _dot(tri, hi) + _dot(tri, mid) + _dot(tri, lo) + carry_ref[0:1, :]
    carry_ref[0:1, :] = f[tm - 1:tm, :]
    nhi, nmid, nlo = _split3(f * -LOG2E)
    parts = (nhi.astype(_F32) + pltpu.roll(nmid.astype(_F32), N_HEADS, 1)
             + pltpu.roll(nlo.astype(_F32), 2 * N_HEADS, 1)).astype(_BF16)
    ka = _dot(z, wka_ref[...]) + _dot(parts, scat_ref[...])
    _store_heads(ka_ref, ka)
    vt_ref[0, :, 0, :HEAD_DIM, :] = vt.astype(_BF16).reshape(N_HEADS, HEAD_DIM, tm)
    extra = lax.broadcasted_iota(jnp.int32, (N_HEADS, VT_ROWS - HEAD_DIM, tm), 1) == 0
    vt_ref[0, :, 0, HEAD_DIM:, :] = extra.astype(_F32).astype(_BF16)
    xq = _rms(x, gq_ref[...]).astype(_BF16)
    _store_heads(qa_ref, _dot(xq, wqa_ref[...]) + ones_ref[...])


def _q_prompt_body(h_ref, gq_ref, wqa_ref, ones_ref, qa_ref):
    xq = _rms(h_ref[0], gq_ref[...]).astype(_BF16)
    _store_heads(qa_ref, _dot(xq, wqa_ref[...]) + ones_ref[...])


def _head_major_spec(tm):
    return pl.BlockSpec((1, N_HEADS, tm, HEAD_SLOT), lambda i, t: (i, 0, t, 0))


def _kvq_prompt(h, gkv, gq, w_kt, w_vt, w_f, b_f, w_ka, w_qa, tri, scat, ones):
    b, s, d = h.shape
    tm = ROW_TILE
    nk = s // tm
    consts = (gkv, gq, w_kt, w_vt, w_f, b_f, w_ka, w_qa, tri, scat, ones)
    tile = lambda c: pl.BlockSpec((1, tm, c), lambda i, t: (i, t, 0))
    tile_t = pl.BlockSpec((1, d, tm), lambda i, t: (i, 0, t))
    hm = jax.ShapeDtypeStruct((b, N_HEADS, s, HEAD_SLOT), _BF16)
    return pl.pallas_call(
        _kvq_prompt_body,
        grid=(b, nk),
        in_specs=[tile(d)] + [_const_spec(c.shape) for c in consts],
        out_specs=[tile_t, tile_t, tile(N_HEADS), _head_major_spec(tm),
                   pl.BlockSpec((1, N_HEADS, 1, VT_ROWS, tm), lambda i, t: (i, 0, t, 0, 0)),
                   _head_major_spec(tm)],
        out_shape=[jax.ShapeDtypeStruct((b, d, s), _F32), jax.ShapeDtypeStruct((b, d, s), _F32),
                   jax.ShapeDtypeStruct((b, s, N_HEADS), _F32), hm,
                   jax.ShapeDtypeStruct((b, N_HEADS, nk, VT_ROWS, tm), _BF16), hm],
        scratch_shapes=[pltpu.VMEM((SUBLANES, LANES), _F32)],
        compiler_params=_params(("arbitrary", "arbitrary")),
        name="kvq_prompt",
    )(h, *consts)


def _q_prompt(h, gq, w_qa, ones):
    b, s, d = h.shape
    tm = ROW_TILE
    consts = (gq, w_qa, ones)
    return pl.pallas_call(
        _q_prompt_body,
        grid=(b, s // tm),
        in_specs=[pl.BlockSpec((1, tm, d), lambda i, t: (i, t, 0))] + [_const_spec(c.shape) for c in consts],
        out_specs=_head_major_spec(tm),
        out_shape=jax.ShapeDtypeStruct((b, N_HEADS, s, HEAD_SLOT), _BF16),
        compiler_params=_params(("arbitrary", "arbitrary")),
        name="q_prompt",
    )(h, *consts)


def _kvq_sample_body(kv, *refs):
    if kv:
        (h_ref, gkv_ref, gq_ref, wk_ref, wv_ref, wf_ref, bf_ref, wq_ref,
         k_ref, v_ref, lf_ref, q_ref) = refs
        x = h_ref[...]
        z = _rms(x, gkv_ref[...]).astype(_BF16)
        k_ref[...] = _dot(z, wk_ref[...])
        v_ref[...] = _dot(z, wv_ref[...])
        lf_ref[...] = _log_sigmoid(_dot(z, wf_ref[...]) + bf_ref[...])[:, :N_HEADS]
    else:
        h_ref, gq_ref, wq_ref, q_ref = refs
        x = h_ref[...]
    q_ref[...] = _dot(_rms(x, gq_ref[...]).astype(_BF16), wq_ref[...])


def _kvq_sample(h, gkv, gq, w_k, w_v, w_f, b_f, w_q):
    n, d = h.shape
    args = (h, gkv, gq, w_k, w_v, w_f, b_f, w_q)
    outs = [(n, d), (n, d), (n, N_HEADS), (n, d)]
    return pl.pallas_call(
        functools.partial(_kvq_sample_body, True),
        grid=(1,),
        in_specs=[_const_spec(a.shape) for a in args],
        out_specs=[_const_spec(o) for o in outs],
        out_shape=[jax.ShapeDtypeStruct(o, _F32) for o in outs],
        compiler_params=_params(("arbitrary",)),
        name="kvq_sample",
    )(*args)


def _q_sample(h, gq, w_q):
    n, d = h.shape
    args = (h, gq, w_q)
    return pl.pallas_call(
        functools.partial(_kvq_sample_body, False),
        grid=(1,),
        in_specs=[_const_spec(a.shape) for a in args],
        out_specs=_const_spec((n, d)),
        out_shape=jax.ShapeDtypeStruct((n, d), _F32),
        compiler_params=_params(("arbitrary",)),
        name="q_sample",
    )(*args)


def _flash_body(q_ref, k_ref, vt_ref, o_ref, s0_ref, s1_ref, m_ref, acc_ref):
    qi = pl.program_id(2)
    tq = q_ref.shape[2]
    tk = vt_ref.shape[4]
    slots = (s0_ref, s1_ref)
    halves = [slice(c, c + FLASH_Q_SPLIT) for c in range(0, tq, FLASH_Q_SPLIT)]

    def scores(hd, j, slot):
        kb = k_ref[0, hd, pl.ds(pl.multiple_of(j * tk, tk), tk), :]
        for cols in halves:
            slots[slot][hd, :, cols] = _dot_nt(kb, q_ref[0, hd, cols, :])

    def update(hd, j, slot, masked):
        for cols in halves:
            keys = slice(0, min(cols.stop, tk)) if masked else slice(0, tk)
            st = slots[slot][hd, keys, cols]
            if masked:
                key_pos = lax.broadcasted_iota(jnp.int32, st.shape, 0)
                qry_pos = lax.broadcasted_iota(jnp.int32, st.shape, 1) + cols.start
                st = jnp.where(key_pos <= qry_pos, st, -jnp.inf)
            m = m_ref[hd, :, cols]
            m_new = jnp.maximum(m, jnp.max(st, axis=0, keepdims=True))
            p = jnp.exp2(st - m_new).astype(_BF16)
            acc_ref[hd, :, cols] = (jnp.exp2(m - m_new) * acc_ref[hd, :, cols]
                                    + _dot(vt_ref[0, hd, j, :, keys], p))
            m_ref[hd, :, cols] = m_new

    m_ref[...] = jnp.full_like(m_ref, -jnp.inf)
    acc_ref[...] = jnp.zeros_like(acc_ref)
    for hd in range(HEAD_GROUP):
        scores(hd, 0, 0)

    def pair(jj, carry):
        j = 2 * jj
        for hd in range(HEAD_GROUP):
            scores(hd, j + 1, 1)
            update(hd, j, 0, False)
        for hd in range(HEAD_GROUP):
            scores(hd, j + 2, 0)
            update(hd, j + 1, 1, False)
        return carry

    lax.fori_loop(0, qi // 2, pair, 0)

    @pl.when(qi % 2 == 0)
    def _():
        for hd in range(HEAD_GROUP):
            update(hd, qi, 0, True)

    @pl.when(qi % 2 == 1)
    def _():
        for hd in range(HEAD_GROUP):
            scores(hd, qi, 1)
            update(hd, qi - 1, 0, False)
        for hd in range(HEAD_GROUP):
            update(hd, qi, 1, True)

    outs = [acc_ref[hd, :HEAD_DIM, :] * (1.0 / acc_ref[hd, HEAD_DIM:HEAD_DIM + 1, :])
            for hd in range(HEAD_GROUP)]
    o_ref[0] = jnp.concatenate(outs, axis=0).T.astype(o_ref.dtype)


def _flash(qa, ka, vt):
    b, _, s, _ = qa.shape
    nk, tk = vt.shape[2], vt.shape[4]
    tq = Q_TILE
    assert tq == tk
    hg = HEAD_GROUP
    return pl.pallas_call(
        _flash_body,
        grid=(b, N_HEADS // hg, s // tq),
        in_specs=[
            pl.BlockSpec((1, hg, tq, HEAD_SLOT), lambda i, g, t: (i, g, t, 0)),
            pl.BlockSpec((1, hg, s, HEAD_SLOT), lambda i, g, t: (i, g, 0, 0)),
            pl.BlockSpec((1, hg, nk, VT_ROWS, tk), lambda i, g, t: (i, g, 0, 0, 0)),
        ],
        out_specs=pl.BlockSpec((1, tq, hg * HEAD_DIM), lambda i, g, t: (i, t, g)),
        out_shape=jax.ShapeDtypeStruct((b, s, D_MODEL), _BF16),
        scratch_shapes=[pltpu.VMEM((hg, tk, tq), _F32), pltpu.VMEM((hg, tk, tq), _F32),
                        pltpu.VMEM((hg, 1, tq), _F32), pltpu.VMEM((hg, VT_ROWS, tq), _F32)],
        compiler_params=_params(("arbitrary", "arbitrary", "arbitrary")),
        name="flash_prompt",
    )(qa, ka, vt)


def _page_cumsum_body(x_ref, triu_ref, o_ref):
    x = x_ref[...].reshape(-1, PAGE)
    hi, mid, lo = _split3(x)
    tri = triu_ref[...]
    o_ref[...] = (_dot(hi, tri) + _dot(mid, tri) + _dot(lo, tri)).reshape(o_ref.shape)


def _page_cumsum(xt, triu, pages):
    n = xt.shape[0]
    spec = pl.BlockSpec((pages, N_HEADS, PAGE), lambda i: (i, 0, 0))
    return pl.pallas_call(
        _page_cumsum_body,
        grid=(n // pages,),
        in_specs=[spec, _const_spec(triu.shape)],
        out_specs=spec,
        out_shape=jax.ShapeDtypeStruct(xt.shape, _F32),
        compiler_params=_params(("arbitrary",)),
        name="page_cumsum",
    )(xt, triu)


def _expand_heads(x):
    n = x.shape[1]
    return jnp.concatenate(
        [jnp.broadcast_to(x[hd:hd + 1, :], (SAMPLE_SEQ, n)) for hd in range(N_HEADS)], axis=0)


class _SampleAttention:
    ROWS = N_HEADS * SAMPLE_SEQ
    SCRATCH = [pltpu.VMEM((ROWS, D_MODEL), _BF16), pltpu.VMEM((ROWS, 1), _F32),
               pltpu.VMEM((ROWS, 1), _F32), pltpu.VMEM((ROWS, D_MODEL), _F32)]

    def __init__(self, qbd_ref, m_ref, l_ref, acc_ref):
        self.qbd_ref, self.m_ref, self.l_ref, self.acc_ref = qbd_ref, m_ref, l_ref, acc_ref
        head_of_row = lax.broadcasted_iota(jnp.int32, (self.ROWS, D_MODEL), 0) // SAMPLE_SEQ
        head_of_col = lax.broadcasted_iota(jnp.int32, (self.ROWS, D_MODEL), 1) // HEAD_DIM
        self.blockdiag = head_of_row == head_of_col

    def start(self, q):
        self.qbd_ref[...] = jnp.where(self.blockdiag, jnp.concatenate([q] * N_HEADS, axis=0),
                                      0.0).astype(_BF16)
        self.m_ref[...] = jnp.full_like(self.m_ref, -jnp.inf)
        self.l_ref[...] = jnp.zeros_like(self.l_ref)
        self.acc_ref[...] = jnp.zeros_like(self.acc_ref)

    def _attend(self, st, v, v_is_transposed):
        m = self.m_ref[...]
        m_new = jnp.maximum(m, jnp.max(st, axis=1, keepdims=True))
        alpha = jnp.exp(m - m_new)
        p = jnp.exp(st - m_new)
        self.l_ref[...] = alpha * self.l_ref[...] + jnp.sum(p, axis=1, keepdims=True)
        pv = _dot_nt(p.astype(_BF16), v) if v_is_transposed else _dot(p.astype(_BF16), v)
        self.acc_ref[...] = alpha * self.acc_ref[...] + pv
        self.m_ref[...] = m_new

    def past(self, kt, vt, f):
        self._attend(_dot(self.qbd_ref[...], kt) - _expand_heads(f), vt, True)

    def finish(self, k_new, v_new, f_new):
        pad = jnp.zeros((PAGE - SAMPLE_SEQ, D_MODEL), _F32)
        kn = jnp.concatenate([k_new, pad], axis=0).astype(_BF16)
        vn = jnp.concatenate([v_new, pad], axis=0).astype(_BF16)
        st = _dot_nt(self.qbd_ref[...], kn) - _expand_heads(f_new)
        key = lax.broadcasted_iota(jnp.int32, (self.ROWS, PAGE), 1)
        qpos = lax.broadcasted_iota(jnp.int32, (self.ROWS, PAGE), 0) % SAMPLE_SEQ
        self._attend(jnp.where(key <= qpos, st, -jnp.inf), vn, False)
        full = jnp.where(self.blockdiag, self.acc_ref[...] * (1.0 / self.l_ref[...]), 0.0)
        out = full[0:SAMPLE_SEQ, :]
        for hd in range(1, N_HEADS):
            out = out + full[hd * SAMPLE_SEQ:(hd + 1) * SAMPLE_SEQ, :]
        return out


def _gather_attend_body(pt_ref, q_ref, *refs):
    npg = PAGES_PER_STEP
    k_refs = refs[:npg]
    v_refs = refs[npg:2 * npg]
    f_refs = refs[2 * npg:3 * npg]
    kn_ref, vn_ref, fn_ref, kc_ref, vc_ref, fo_ref, o_ref, carry_ref = refs[3 * npg:3 * npg + 8]
    attn = _SampleAttention(*refs[3 * npg + 8:])
    j = pl.program_id(1)

    @pl.when(j == 0)
    def _():
        carry_ref[...] = jnp.zeros_like(carry_ref)
        attn.start(q_ref[0])

    c = carry_ref[...]
    for i in range(npg):
        keys = slice(i * PAGE, (i + 1) * PAGE)
        kc_ref[0, :, keys] = k_refs[i][0].reshape(D_MODEL, PAGE).astype(_BF16)
        vc_ref[0, :, keys] = v_refs[i][0].reshape(D_MODEL, PAGE).astype(_BF16)
        f = f_refs[i][0] + c
        fo_ref[0, :, keys] = f
        c = f[:, PAGE - 1:PAGE]
    carry_ref[...] = c
    attn.past(kc_ref[0], vc_ref[0], fo_ref[0])

    @pl.when(j == pl.num_programs(1) - 1)
    def _():
        o_ref[0] = attn.finish(kn_ref[0], vn_ref[0], fn_ref[0] + c).astype(o_ref.dtype)


def _gather_attend(page_table, q, cache_kt, cache_vt, floc, k_new, v_new, floc_new):
    bd, n_pages = page_table.shape
    t, d = q.shape[1:]
    assert t == SAMPLE_SEQ
    npg = PAGES_PER_STEP
    past = n_pages * PAGE
    pt = page_table.reshape(-1)

    def page_spec(shape, i):
        nd = len(shape) - 1
        return pl.BlockSpec(shape, lambda b, j, pt_ref: (pt_ref[b * n_pages + j * npg + i],) + (0,) * nd)

    per_b = lambda shape: pl.BlockSpec(shape, lambda b, j, pt_ref: (b, 0, 0))
    kv_block = (1, N_HEADS, HEAD_DIM, PAGE)
    in_specs = ([per_b((1, t, d))]
                + [page_spec(kv_block, i) for i in range(npg)]
                + [page_spec(kv_block, i) for i in range(npg)]
                + [page_spec((1, N_HEADS, PAGE), i) for i in range(npg)]
                + [per_b((1, t, d)), per_b((1, t, d)), per_b((1, N_HEADS, PAGE))])
    dense = pl.BlockSpec((1, D_MODEL, npg * PAGE), lambda b, j, pt_ref: (b, 0, j))
    grid_spec = pltpu.PrefetchScalarGridSpec(
        num_scalar_prefetch=1,
        grid=(bd, n_pages // npg),
        in_specs=in_specs,
        out_specs=[dense, dense, pl.BlockSpec((1, N_HEADS, npg * PAGE), lambda b, j, pt_ref: (b, 0, j)),
                   per_b((1, t, d))],
        scratch_shapes=[pltpu.VMEM((N_HEADS, 1), _F32)] + _SampleAttention.SCRATCH,
    )
    return pl.pallas_call(
        _gather_attend_body,
        grid_spec=grid_spec,
        out_shape=[jax.ShapeDtypeStruct((bd, D_MODEL, past), _BF16),
                   jax.ShapeDtypeStruct((bd, D_MODEL, past), _BF16),
                   jax.ShapeDtypeStruct((bd, N_HEADS, past), _F32),
                   jax.ShapeDtypeStruct((bd, t, d), _BF16)],
        compiler_params=_params(("arbitrary", "arbitrary")),
        name="gather_attend",
    )(pt, q, *([cache_kt] * npg), *([cache_vt] * npg), *([floc] * npg), k_new, v_new, floc_new)


def _paged_body(q_ref, k_ref, v_ref, f_ref, kn_ref, vn_ref, fn_ref, o_ref, *scratch):
    attn = _SampleAttention(*scratch)
    j = pl.program_id(1)

    @pl.when(j == 0)
    def _():
        attn.start(q_ref[0])

    f = f_ref[0]
    attn.past(k_ref[0], v_ref[0], f)

    @pl.when(j == pl.num_programs(1) - 1)
    def _():
        f_new = fn_ref[0] + f[:, f.shape[1] - 1:]
        o_ref[0] = attn.finish(kn_ref[0], vn_ref[0], f_new).astype(o_ref.dtype)


def _paged(q, kc, vc, f_past, k_new, v_new, floc_new):
    bd, t, d = q.shape
    assert t == SAMPLE_SEQ
    past = kc.shape[2]
    tk = PAGED_KEYS
    per_b = lambda shape: pl.BlockSpec(shape, lambda b, j: (b, 0, 0))
    in_specs = [per_b((1, t, d)),
                pl.BlockSpec((1, d, tk), lambda b, j: (b, 0, j)),
                pl.BlockSpec((1, d, tk), lambda b, j: (b, 0, j)),
                pl.BlockSpec((1, N_HEADS, tk), lambda b, j: (b, 0, j)),
                per_b((1, t, d)), per_b((1, t, d)), per_b((1, N_HEADS, PAGE))]
    return pl.pallas_call(
        _paged_body,
        grid=(bd, past // tk),
        in_specs=in_specs,
        out_specs=per_b((1, t, d)),
        out_shape=jax.ShapeDtypeStruct((bd, t, d), _BF16),
        scratch_shapes=_SampleAttention.SCRATCH,
        compiler_params=_params(("arbitrary", "arbitrary")),
        name="paged_sample",
    )(q, kc, vc, f_past, k_new, v_new, floc_new)


def _head_slots(w):
    w = w.reshape(D_MODEL, N_HEADS, HEAD_DIM)
    w = jnp.pad(w, ((0, 0), (0, 0), (0, HEAD_SLOT - HEAD_DIM)))
    return w.reshape(D_MODEL, N_HEADS * HEAD_SLOT)


def _constants():
    tri = np.tril(np.ones((ROW_TILE, ROW_TILE), np.float32))
    triu = np.triu(np.ones((PAGE, PAGE), np.float32))
    scat = np.zeros((LANES, N_HEADS * HEAD_SLOT), np.float32)
    ones = np.zeros((1, N_HEADS * HEAD_SLOT), np.float32)
    for part in range(3):
        for hd in range(N_HEADS):
            scat[part * N_HEADS + hd, hd * HEAD_SLOT + BIAS_LANE + part] = 1.0
            ones[0, hd * HEAD_SLOT + BIAS_LANE + part] = 1.0
    return (jnp.asarray(tri, _BF16), jnp.asarray(triu, _BF16), jnp.asarray(scat, _BF16),
            jnp.asarray(ones, _F32))


def kernel(x_prompt, x_sample, state_conv, cache_k, cache_v, cache_logf, page_table, p_prompt, p_sample,
           norm_g, w_in, w_conv, w_out, g_kv, w_k, w_v, w_f, b_f, w_q, w_o,
           w_gate, w_up, w_down, w_ple, w_ple_gate):
    b, s, d = x_prompt.shape
    bd, t, _ = x_sample.shape
    depth = norm_g.shape[0]
    bf = lambda w: w.astype(_BF16)
    tri, triu, scat, ones = _constants()

    w_in_b, w_out_b, w_o_b = bf(w_in), bf(w_out), bf(w_o)
    w_gate_b, w_up_b, w_down_b = bf(w_gate), bf(w_up), bf(w_down)
    w_ple_b, w_pg_b = bf(w_ple), bf(w_ple_gate)
    w_k_b, w_v_b, w_kt_b, w_vt_b = bf(w_k), bf(w_v), bf(w_k.T), bf(w_v.T)
    w_ka = bf(_head_slots(w_k))
    w_q_scaled = w_q * SCALE
    w_qa = [bf(_head_slots(w_q_scaled[j] * LOG2E)) for j in range(w_q.shape[0])]
    w_qs = bf(w_q_scaled)
    w_f_pad = bf(jnp.pad(w_f, ((0, 0), (0, LANES - N_HEADS))))
    b_f_pad = jnp.pad(b_f, (0, LANES - N_HEADS)).reshape(1, LANES)
    gkv = g_kv.reshape(1, d)

    hp = x_prompt
    conv_prompt = []
    for i in range(N_CONV_LAYERS):
        hp, st = _conv_prompt(hp, norm_g[i], w_in_b[i], w_conv[i], w_out_b[i])
        conv_prompt.append(st)
        hp = _ffn(hp.reshape(b * s, d), None, p_prompt[i].reshape(b * s, PLE_DIM), norm_g[i], None,
                  w_gate_b[i], w_up_b[i], w_down_b[i], w_ple_b[i], w_pg_b[i]).reshape(b, s, d)
    kt_p, vt_p, lf_p, ka, vt, qa = _kvq_prompt(
        hp, gkv, norm_g[N_CONV_LAYERS, 0:1], w_kt_b, w_vt_b, w_f_pad, b_f_pad, w_ka, w_qa[0],
        tri, scat, ones)
    heads_last = lambda xt: jnp.transpose(xt.reshape(b, N_HEADS, HEAD_DIM, s), (0, 3, 1, 2))
    for i in range(N_CONV_LAYERS, depth):
        jl = i - N_CONV_LAYERS
        if jl > 0:
            qa = _q_prompt(hp, norm_g[i, 0:1], w_qa[jl], ones)
        o = _flash(qa, ka, vt)
        hp = _ffn(hp.reshape(b * s, d), o.reshape(b * s, d), p_prompt[i].reshape(b * s, PLE_DIM),
                  norm_g[i], w_o_b[jl], w_gate_b[i], w_up_b[i], w_down_b[i], w_ple_b[i],
                  w_pg_b[i]).reshape(b, s, d)

    n = bd * t
    hs = x_sample.reshape(n, d)
    zeros_rows = jnp.zeros((bd, t - 2, d), _F32)
    conv_sample = []
    for i in range(N_CONV_LAYERS):
        st = state_conv[i]
        fix1 = jnp.concatenate([st[:, 1:2], st[:, 1:2], zeros_rows], axis=1).reshape(n, d)
        fix2 = jnp.concatenate([st, zeros_rows], axis=1).reshape(n, d)
        hs, cu = _conv_sample(hs, fix1, fix2, norm_g[i], w_in_b[i], w_conv[i], w_out_b[i])
        conv_sample.append(cu.reshape(bd, t, d)[:, t - 2:])
        hs = _ffn(hs, None, p_sample[i].reshape(n, PLE_DIM), norm_g[i], None,
                  w_gate_b[i], w_up_b[i], w_down_b[i], w_ple_b[i], w_pg_b[i])
    k_s, v_s, lf_s, q_s = _kvq_sample(hs, gkv, norm_g[N_CONV_LAYERS, 0:1], w_k_b, w_v_b, w_f_pad,
                                      b_f_pad, w_qs[0])
    floc = _page_cumsum(jnp.swapaxes(cache_logf, 1, 2), triu, PREPASS_PAGES)
    lf_new = jnp.pad(jnp.swapaxes(lf_s.reshape(bd, t, N_HEADS), 1, 2), ((0, 0), (0, 0), (0, PAGE - t)))
    floc_new = _page_cumsum(lf_new, triu, bd)
    to_key_minor = lambda c: jnp.transpose(c, (0, 2, 3, 1))
    k_new = k_s.reshape(bd, t, d)
    v_new = v_s.reshape(bd, t, d)
    for i in range(N_CONV_LAYERS, depth):
        jl = i - N_CONV_LAYERS
        if jl == 0:
            kc, vc, f_past, o = _gather_attend(page_table, q_s.reshape(bd, t, d), to_key_minor(cache_k),
                                               to_key_minor(cache_v), floc, k_new, v_new, floc_new)
        else:
            q_s = _q_sample(hs, norm_g[i, 0:1], w_qs[jl])
            o = _paged(q_s.reshape(bd, t, d), kc, vc, f_past, k_new, v_new, floc_new)
        hs = _ffn(hs, o.reshape(n, d), p_sample[i].reshape(n, PLE_DIM), norm_g[i], w_o_b[jl],
                  w_gate_b[i], w_up_b[i], w_down_b[i], w_ple_b[i], w_pg_b[i])

    return (hp, hs.reshape(bd, t, d), jnp.stack(conv_prompt, axis=0),
            heads_last(kt_p), heads_last(vt_p), lf_p,
            jnp.stack(conv_sample, axis=0),
            k_s.reshape(bd, t, N_HEADS, HEAD_DIM), v_s.reshape(bd, t, N_HEADS, HEAD_DIM),
            lf_s.reshape(bd, t, N_HEADS))
```

```python
import functools

import jax
import jax.numpy as jnp
import numpy as np
from jax import lax
from jax.experimental import pallas as pl
from jax.experimental.pallas import tpu as pltpu

D_MODEL = 1024
N_HEADS = 16
HEAD_DIM = 64
D_FF = 2816
PLE_DIM = 256
EPS = 1e-6
N_CONV_LAYERS = 2
PAGE = 128
LANES = 128
SUBLANES = 8
SAMPLE_SEQ = 8
HEAD_SLOT = 128
BIAS_LANE = HEAD_DIM
SCALE = HEAD_DIM ** -0.5
LOG2E = 1.4426950408889634
VT_ROWS = HEAD_DIM + 16

ROW_TILE = 512
FFN_TILE = 256
Q_TILE = 512
FLASH_Q_SPLIT = 256
HEAD_GROUP = 4
PAGES_PER_STEP = 8
PAGED_KEYS = 2048
PREPASS_PAGES = 128
VMEM_LIMIT = 56 * 1024 * 1024

_F32 = jnp.float32
_BF16 = jnp.bfloat16


def _rms(x, g):
    return x * lax.rsqrt(jnp.mean(x * x, axis=-1, keepdims=True) + EPS) * g


def _dot(a, b):
    return jnp.dot(a, b, preferred_element_type=_F32)


def _dot_nt(a, b):
    return lax.dot_general(a, b, (((1,), (1,)), ((), ())), preferred_element_type=_F32)


def _split3(x):
    hi = x.astype(_BF16)
    r = x - hi.astype(_F32)
    mid = r.astype(_BF16)
    lo = (r - mid.astype(_F32)).astype(_BF16)
    return hi, mid, lo


def _const_spec(shape):
    nd = len(shape)
    return pl.BlockSpec(shape, lambda *_: (0,) * nd, pipeline_mode=pl.Buffered(1))


def _layer_spec(shape, layer):
    nd = len(shape) - 1
    return pl.BlockSpec((None,) + tuple(shape[1:]), lambda *_: (layer,) + (0,) * nd,
                        pipeline_mode=pl.Buffered(1))


def _params(sem):
    return pltpu.CompilerParams(dimension_semantics=sem, vmem_limit_bytes=VMEM_LIMIT)


def _conv_body(sample, *refs):
    if sample:
        (h_ref, fix1_ref, fix2_ref, g_ref, win_ref, wconv_ref, wout_ref, out_ref, cu_ref) = refs
    else:
        (h_ref, g_ref, win_ref, wconv_ref, wout_ref, out_ref, state_ref, carry_ref) = refs

        @pl.when(pl.program_id(1) == 0)
        def _():
            carry_ref[...] = jnp.zeros_like(carry_ref)

    x = h_ref[...].reshape(h_ref.shape[-2:])
    tm = x.shape[0]
    xn = _rms(x, g_ref[0:1, :]).astype(_BF16)
    proj = _dot(xn, win_ref[...])
    bg = proj[:, :D_MODEL]
    cu = proj[:, D_MODEL:2 * D_MODEL] * proj[:, 2 * D_MODEL:]
    s1 = pltpu.roll(cu, 1, 0)
    s2 = pltpu.roll(cu, 2, 0)
    row = lax.broadcasted_iota(jnp.int32, (tm, 1), 0)
    if sample:
        pos = row % SAMPLE_SEQ
        s1 = jnp.where(pos == 0, fix1_ref[...], s1)
        s2 = jnp.where(pos < 2, fix2_ref[...], s2)
    else:
        prev2 = carry_ref[0:1, :]
        prev1 = carry_ref[1:2, :]
        s1 = jnp.where(row == 0, prev1, s1)
        s2 = jnp.where(row == 0, prev2, jnp.where(row == 1, prev1, s2))
    conv = wconv_ref[0:1, :] * s2 + wconv_ref[1:2, :] * s1 + wconv_ref[2:3, :] * cu
    y = _dot((bg * conv).astype(_BF16), wout_ref[...])
    out = x + _rms(y, g_ref[1:2, :])
    out_ref[...] = out.reshape(out_ref.shape)
    if sample:
        cu_ref[...] = cu
    else:
        tail = cu[tm - 2:tm, :]
        carry_ref[0:2, :] = tail
        state_ref[0] = tail


def _conv_prompt(h, g, w_in, w_conv, w_out, layer):
    b, s, d = h.shape
    tm = ROW_TILE
    return pl.pallas_call(
        functools.partial(_conv_body, False),
        grid=(b, s // tm),
        in_specs=[
            pl.BlockSpec((1, tm, d), lambda i, t: (i, t, 0)),
            _const_spec(g.shape), _layer_spec(w_in.shape, layer), _const_spec(w_conv.shape),
            _layer_spec(w_out.shape, layer),
        ],
        out_specs=[
            pl.BlockSpec((1, tm, d), lambda i, t: (i, t, 0)),
            pl.BlockSpec((1, 2, d), lambda i, t: (i, 0, 0)),
        ],
        out_shape=[jax.ShapeDtypeStruct(h.shape, _F32), jax.ShapeDtypeStruct((b, 2, d), _F32)],
        scratch_shapes=[pltpu.VMEM((SUBLANES, d), _F32)],
        compiler_params=_params(("arbitrary", "arbitrary")),
        name="conv_prompt",
    )(h, g, w_in, w_conv, w_out)


def _conv_sample(h, fix1, fix2, g, w_in, w_conv, w_out, layer):
    n, d = h.shape
    args = (h, fix1, fix2, g, w_in, w_conv, w_out)
    specs = [_const_spec(a.shape) for a in args]
    specs[4] = _layer_spec(w_in.shape, layer)
    specs[6] = _layer_spec(w_out.shape, layer)
    return pl.pallas_call(
        functools.partial(_conv_body, True),
        grid=(1,),
        in_specs=specs,
        out_specs=[_const_spec((n, d)), _const_spec((n, d))],
        out_shape=[jax.ShapeDtypeStruct((n, d), _F32), jax.ShapeDtypeStruct((n, d), _F32)],
        compiler_params=_params(("arbitrary",)),
        name="conv_sample",
    )(*args)


def _ffn_body(attn, *refs):
    if attn:
        (h_ref, o_ref, p_ref, g_ref, wo_ref, wg_ref, wu_ref, wd_ref, wple_ref, wpg_ref, out_ref) = refs
    else:
        (h_ref, p_ref, g_ref, wg_ref, wu_ref, wd_ref, wple_ref, wpg_ref, out_ref) = refs
    x = h_ref[...]
    if attn:
        x = x + _rms(_dot(o_ref[...], wo_ref[...]), g_ref[1:2, :])
    xn = _rms(x, g_ref[2:3, :]).astype(_BF16)
    gate = _dot(xn, wg_ref[...])
    up = _dot(xn, wu_ref[...])
    act = (gate * jax.nn.sigmoid(gate) * up).astype(_BF16)
    x = x + _rms(_dot(act, wd_ref[...]), g_ref[3:4, :])
    pg = jax.nn.sigmoid(_dot(x.astype(_BF16), wpg_ref[...]))
    pe = _dot(p_ref[...].astype(_BF16), wple_ref[...])
    out_ref[...] = x + pg * pe


def _ffn(h, o, p, layer, g, w_o, attn_layer, w_gate, w_up, w_down, w_ple, w_ple_gate):
    n, d = h.shape
    tm = min(FFN_TILE, n)
    row = lambda c: pl.BlockSpec((tm, c), lambda t: (t, 0))
    p_spec = pl.BlockSpec((None, tm, p.shape[2]), lambda t: (layer, t, 0))
    attn = o is not None
    args = [h] + ([o] if attn else []) + [p, g] + ([w_o] if attn else []) + [
        w_gate, w_up, w_down, w_ple, w_ple_gate]
    specs = [row(d)] + ([row(d)] if attn else []) + [p_spec, _const_spec(g.shape)] + (
        [_layer_spec(w_o.shape, attn_layer)] if attn else []) + [
        _layer_spec(w.shape, layer) for w in (w_gate, w_up, w_down, w_ple, w_ple_gate)]
    return pl.pallas_call(
        functools.partial(_ffn_body, attn),
        grid=(n // tm,),
        in_specs=specs,
        out_specs=row(d),
        out_shape=jax.ShapeDtypeStruct((n, d), _F32),
        compiler_params=_params(("arbitrary",)),
        name="ffn_attn" if attn else "ffn",
    )(*args)


def _log_sigmoid(x):
    return jnp.minimum(x, 0.0) - jnp.log1p(jnp.exp(-jnp.abs(x)))


def _store_heads(ref, x):
    for hd in range(N_HEADS):
        ref[0, hd] = x[:, hd * HEAD_SLOT:(hd + 1) * HEAD_SLOT].astype(ref.dtype)


def _kvq_prompt_body(h_ref, gkv_ref, gq_ref, wkt_ref, wvt_ref, wf_ref, bf_ref, wka_ref,
                     wqa_ref, tri_ref, scat_ref, ones_ref,
                     kt_ref, vto_ref, lf_ref, ka_ref, vt_ref, qa_ref, carry_ref):
    @pl.when(pl.program_id(1) == 0)
    def _():
        carry_ref[...] = jnp.zeros_like(carry_ref)

    x = h_ref[0]
    tm = x.shape[0]
    z = _rms(x, gkv_ref[...]).astype(_BF16)
    kt_ref[0] = _dot_nt(wkt_ref[...], z)
    vt = _dot_nt(wvt_ref[...], z)
    vto_ref[0] = vt
    lf = _log_sigmoid(_dot(z, wf_ref[...]) + bf_ref[...])
    lane = lax.broadcasted_iota(jnp.int32, (1, LANES), 1)
    lf = jnp.where(lane < N_HEADS, lf, 0.0)
    lf_ref[0] = lf[:, :N_HEADS]
    hi, mid, lo = _split3(lf)
    tri = tri_ref[...]
    f = _dot(tri, hi) + _dot(tri, mid) + _dot(tri, lo) + carry_ref[0:1, :]
    carry_ref[0:1, :] = f[tm - 1:tm, :]
    nhi, nmid, nlo = _split3(f * -LOG2E)
    parts = (nhi.astype(_F32) + pltpu.roll(nmid.astype(_F32), N_HEADS, 1)
             + pltpu.roll(nlo.astype(_F32), 2 * N_HEADS, 1)).astype(_BF16)
    ka = _dot(z, wka_ref[...]) + _dot(parts, scat_ref[...])
    _store_heads(ka_ref, ka)
    vt_ref[0, :, 0, :HEAD_DIM, :] = vt.astype(_BF16).reshape(N_HEADS, HEAD_DIM, tm)
    extra = lax.broadcasted_iota(jnp.int32, (N_HEADS, VT_ROWS - HEAD_DIM, tm), 1) == 0
    vt_ref[0, :, 0, HEAD_DIM:, :] = extra.astype(_F32).astype(_BF16)
    xq = _rms(x, gq_ref[...]).astype(_BF16)
    _store_heads(qa_ref, _dot(xq, wqa_ref[...]) + ones_ref[...])


def _q_prompt_body(h_ref, gq_ref, wqa_ref, ones_ref, qa_ref):
    xq = _rms(h_ref[0], gq_ref[...]).astype(_BF16)
    _store_heads(qa_ref, _dot(xq, wqa_ref[...]) + ones_ref[...])


def _head_major_spec(tm):
    return pl.BlockSpec((1, N_HEADS, tm, HEAD_SLOT), lambda i, t: (i, 0, t, 0))


def _kvq_prompt(h, gkv, gq, w_kt, w_vt, w_f, b_f, w_ka, w_qa, tri, scat, ones):
    b, s, d = h.shape
    tm = ROW_TILE
    nk = s // tm
    consts = (gkv, gq, w_kt, w_vt, w_f, b_f, w_ka, w_qa, tri, scat, ones)
    tile = lambda c: pl.BlockSpec((1, tm, c), lambda i, t: (i, t, 0))
    tile_t = pl.BlockSpec((1, d, tm), lambda i, t: (i, 0, t))
    hm = jax.ShapeDtypeStruct((b, N_HEADS, s, HEAD_SLOT), _BF16)
    return pl.pallas_call(
        _kvq_prompt_body,
        grid=(b, nk),
        in_specs=[tile(d)] + [_const_spec(c.shape) for c in consts],
        out_specs=[tile_t, tile_t, tile(N_HEADS), _head_major_spec(tm),
                   pl.BlockSpec((1, N_HEADS, 1, VT_ROWS, tm), lambda i, t: (i, 0, t, 0, 0)),
                   _head_major_spec(tm)],
        out_shape=[jax.ShapeDtypeStruct((b, d, s), _F32), jax.ShapeDtypeStruct((b, d, s), _F32),
                   jax.ShapeDtypeStruct((b, s, N_HEADS), _F32), hm,
                   jax.ShapeDtypeStruct((b, N_HEADS, nk, VT_ROWS, tm), _BF16), hm],
        scratch_shapes=[pltpu.VMEM((SUBLANES, LANES), _F32)],
        compiler_params=_params(("arbitrary", "arbitrary")),
        name="kvq_prompt",
    )(h, *consts)


def _q_prompt(h, gq, w_qa, ones):
    b, s, d = h.shape
    tm = ROW_TILE
    consts = (gq, w_qa, ones)
    return pl.pallas_call(
        _q_prompt_body,
        grid=(b, s // tm),
        in_specs=[pl.BlockSpec((1, tm, d), lambda i, t: (i, t, 0))] + [_const_spec(c.shape) for c in consts],
        out_specs=_head_major_spec(tm),
        out_shape=jax.ShapeDtypeStruct((b, N_HEADS, s, HEAD_SLOT), _BF16),
        compiler_params=_params(("arbitrary", "arbitrary")),
        name="q_prompt",
    )(h, *consts)


def _kvq_sample_body(kv, *refs):
    if kv:
        (h_ref, gkv_ref, gq_ref, wk_ref, wv_ref, wf_ref, bf_ref, wq_ref,
         k_ref, v_ref, lf_ref, q_ref) = refs
        x = h_ref[...]
        z = _rms(x, gkv_ref[...]).astype(_BF16)
        k_ref[...] = _dot(z, wk_ref[...])
        v_ref[...] = _dot(z, wv_ref[...])
        lf_ref[...] = _log_sigmoid(_dot(z, wf_ref[...]) + bf_ref[...])[:, :N_HEADS]
    else:
        h_ref, gq_ref, wq_ref, q_ref = refs
        x = h_ref[...]
    q_ref[...] = _dot(_rms(x, gq_ref[...]).astype(_BF16), wq_ref[...])


def _kvq_sample(h, gkv, gq, w_k, w_v, w_f, b_f, w_q):
    n, d = h.shape
    args = (h, gkv, gq, w_k, w_v, w_f, b_f, w_q)
    outs = [(n, d), (n, d), (n, N_HEADS), (n, d)]
    return pl.pallas_call(
        functools.partial(_kvq_sample_body, True),
        grid=(1,),
        in_specs=[_const_spec(a.shape) for a in args],
        out_specs=[_const_spec(o) for o in outs],
        out_shape=[jax.ShapeDtypeStruct(o, _F32) for o in outs],
        compiler_params=_params(("arbitrary",)),
        name="kvq_sample",
    )(*args)


def _q_sample(h, gq, w_q):
    n, d = h.shape
    args = (h, gq, w_q)
    return pl.pallas_call(
        functools.partial(_kvq_sample_body, False),
        grid=(1,),
        in_specs=[_const_spec(a.shape) for a in args],
        out_specs=_const_spec((n, d)),
        out_shape=jax.ShapeDtypeStruct((n, d), _F32),
        compiler_params=_params(("arbitrary",)),
        name="q_sample",
    )(*args)


def _flash_body(q_ref, k_ref, vt_ref, o_ref, s0_ref, s1_ref, m_ref, acc_ref):
    qi = pl.program_id(2)
    tq = q_ref.shape[2]
    tk = vt_ref.shape[4]
    slots = (s0_ref, s1_ref)
    halves = [slice(c, c + FLASH_Q_SPLIT) for c in range(0, tq, FLASH_Q_SPLIT)]

    def scores(hd, j, slot):
        kb = k_ref[0, hd, pl.ds(pl.multiple_of(j * tk, tk), tk), :]
        for cols in halves:
            slots[slot][hd, :, cols] = _dot_nt(kb, q_ref[0, hd, cols, :])

    def update(hd, j, slot, masked):
        for cols in halves:
            keys = slice(0, min(cols.stop, tk)) if masked else slice(0, tk)
            st = slots[slot][hd, keys, cols]
            if masked:
                key_pos = lax.broadcasted_iota(jnp.int32, st.shape, 0)
                qry_pos = lax.broadcasted_iota(jnp.int32, st.shape, 1) + cols.start
                st = jnp.where(key_pos <= qry_pos, st, -jnp.inf)
            m = m_ref[hd, :, cols]
            m_new = jnp.maximum(m, jnp.max(st, axis=0, keepdims=True))
            p = jnp.exp2(st - m_new).astype(_BF16)
            acc_ref[hd, :, cols] = (jnp.exp2(m - m_new) * acc_ref[hd, :, cols]
                                    + _dot(vt_ref[0, hd, j, :, keys], p))
            m_ref[hd, :, cols] = m_new

    m_ref[...] = jnp.full_like(m_ref, -jnp.inf)
    acc_ref[...] = jnp.zeros_like(acc_ref)
    for hd in range(HEAD_GROUP):
        scores(hd, 0, 0)

    def pair(jj, carry):
        j = 2 * jj
        for hd in range(HEAD_GROUP):
            scores(hd, j + 1, 1)
            update(hd, j, 0, False)
        for hd in range(HEAD_GROUP):
            scores(hd, j + 2, 0)
            update(hd, j + 1, 1, False)
        return carry

    lax.fori_loop(0, qi // 2, pair, 0)

    @pl.when(qi % 2 == 0)
    def _():
        for hd in range(HEAD_GROUP):
            update(hd, qi, 0, True)

    @pl.when(qi % 2 == 1)
    def _():
        for hd in range(HEAD_GROUP):
            scores(hd, qi, 1)
            update(hd, qi - 1, 0, False)
        for hd in range(HEAD_GROUP):
            update(hd, qi, 1, True)

    outs = [acc_ref[hd, :HEAD_DIM, :] * (1.0 / acc_ref[hd, HEAD_DIM:HEAD_DIM + 1, :])
            for hd in range(HEAD_GROUP)]
    o_ref[0] = jnp.concatenate(outs, axis=0).T.astype(o_ref.dtype)


def _flash(qa, ka, vt):
    b, _, s, _ = qa.shape
    nk, tk = vt.shape[2], vt.shape[4]
    tq = Q_TILE
    assert tq == tk
    hg = HEAD_GROUP
    return pl.pallas_call(
        _flash_body,
        grid=(b, N_HEADS // hg, s // tq),
        in_specs=[
            pl.BlockSpec((1, hg, tq, HEAD_SLOT), lambda i, g, t: (i, g, t, 0)),
            pl.BlockSpec((1, hg, s, HEAD_SLOT), lambda i, g, t: (i, g, 0, 0)),
            pl.BlockSpec((1, hg, nk, VT_ROWS, tk), lambda i, g, t: (i, g, 0, 0, 0)),
        ],
        out_specs=pl.BlockSpec((1, tq, hg * HEAD_DIM), lambda i, g, t: (i, t, g)),
        out_shape=jax.ShapeDtypeStruct((b, s, D_MODEL), _BF16),
        scratch_shapes=[pltpu.VMEM((hg, tk, tq), _F32), pltpu.VMEM((hg, tk, tq), _F32),
                        pltpu.VMEM((hg, 1, tq), _F32), pltpu.VMEM((hg, VT_ROWS, tq), _F32)],
        compiler_params=_params(("arbitrary", "arbitrary", "arbitrary")),
        name="flash_prompt",
    )(qa, ka, vt)


def _page_cumsum_body(x_ref, triu_ref, o_ref):
    x = x_ref[...].reshape(-1, PAGE)
    hi, mid, lo = _split3(x)
    tri = triu_ref[...]
    o_ref[...] = (_dot(hi, tri) + _dot(mid, tri) + _dot(lo, tri)).reshape(o_ref.shape)


def _page_cumsum(xt, triu, pages):
    n = xt.shape[0]
    spec = pl.BlockSpec((pages, N_HEADS, PAGE), lambda i: (i, 0, 0))
    return pl.pallas_call(
        _page_cumsum_body,
        grid=(n // pages,),
        in_specs=[spec, _const_spec(triu.shape)],
        out_specs=spec,
        out_shape=jax.ShapeDtypeStruct(xt.shape, _F32),
        compiler_params=_params(("arbitrary",)),
        name="page_cumsum",
    )(xt, triu)


def _expand_heads(x):
    n = x.shape[1]
    return jnp.concatenate(
        [jnp.broadcast_to(x[hd:hd + 1, :], (SAMPLE_SEQ, n)) for hd in range(N_HEADS)], axis=0)


class _SampleAttention:
    ROWS = N_HEADS * SAMPLE_SEQ
    SCRATCH = [pltpu.VMEM((ROWS, D_MODEL), _BF16), pltpu.VMEM((ROWS, 1), _F32),
               pltpu.VMEM((ROWS, 1), _F32), pltpu.VMEM((ROWS, D_MODEL), _F32)]

    def __init__(self, qbd_ref, m_ref, l_ref, acc_ref):
        self.qbd_ref, self.m_ref, self.l_ref, self.acc_ref = qbd_ref, m_ref, l_ref, acc_ref
        head_of_row = lax.broadcasted_iota(jnp.int32, (self.ROWS, D_MODEL), 0) // SAMPLE_SEQ
        head_of_col = lax.broadcasted_iota(jnp.int32, (self.ROWS, D_MODEL), 1) // HEAD_DIM
        self.blockdiag = head_of_row == head_of_col

    def start(self, q):
        self.qbd_ref[...] = jnp.where(self.blockdiag, jnp.concatenate([q] * N_HEADS, axis=0),
                                      0.0).astype(_BF16)
        self.m_ref[...] = jnp.full_like(self.m_ref, -jnp.inf)
        self.l_ref[...] = jnp.zeros_like(self.l_ref)
        self.acc_ref[...] = jnp.zeros_like(self.acc_ref)

    def _attend(self, st, v, v_is_transposed):
        m = self.m_ref[...]
        m_new = jnp.maximum(m, jnp.max(st, axis=1, keepdims=True))
        alpha = jnp.exp(m - m_new)
        p = jnp.exp(st - m_new)
        self.l_ref[...] = alpha * self.l_ref[...] + jnp.sum(p, axis=1, keepdims=True)
        pv = _dot_nt(p.astype(_BF16), v) if v_is_transposed else _dot(p.astype(_BF16), v)
        self.acc_ref[...] = alpha * self.acc_ref[...] + pv
        self.m_ref[...] = m_new

    def past(self, kt, vt, f):
        self._attend(_dot(self.qbd_ref[...], kt) - _expand_heads(f), vt, True)

    def finish(self, k_new, v_new, f_new):
        pad = jnp.zeros((PAGE - SAMPLE_SEQ, D_MODEL), _F32)
        kn = jnp.concatenate([k_new, pad], axis=0).astype(_BF16)
        vn = jnp.concatenate([v_new, pad], axis=0).astype(_BF16)
        st = _dot_nt(self.qbd_ref[...], kn) - _expand_heads(f_new)
        key = lax.broadcasted_iota(jnp.int32, (self.ROWS, PAGE), 1)
        qpos = lax.broadcasted_iota(jnp.int32, (self.ROWS, PAGE), 0) % SAMPLE_SEQ
        self._attend(jnp.where(key <= qpos, st, -jnp.inf), vn, False)
        full = jnp.where(self.blockdiag, self.acc_ref[...] * (1.0 / self.l_ref[...]), 0.0)
        out = full[0:SAMPLE_SEQ, :]
        for hd in range(1, N_HEADS):
            out = out + full[hd * SAMPLE_SEQ:(hd + 1) * SAMPLE_SEQ, :]
        return out


def _gather_attend_body(pt_ref, q_ref, *refs):
    npg = PAGES_PER_STEP
    k_refs = refs[:npg]
    v_refs = refs[npg:2 * npg]
    f_refs = refs[2 * npg:3 * npg]
    kn_ref, vn_ref, fn_ref, kc_ref, vc_ref, fo_ref, o_ref, carry_ref = refs[3 * npg:3 * npg + 8]
    attn = _SampleAttention(*refs[3 * npg + 8:])
    j = pl.program_id(1)

    @pl.when(j == 0)
    def _():
        carry_ref[...] = jnp.zeros_like(carry_ref)
        attn.start(q_ref[0])

    c = carry_ref[...]
    for i in range(npg):
        keys = slice(i * PAGE, (i + 1) * PAGE)
        kc_ref[0, 0, :, keys] = k_refs[i][0].reshape(D_MODEL, PAGE).astype(_BF16)
        vc_ref[0, 0, :, keys] = v_refs[i][0].reshape(D_MODEL, PAGE).astype(_BF16)
        f = f_refs[i][0] + c
        fo_ref[0, 0, :, keys] = f
        c = f[:, PAGE - 1:PAGE]
    carry_ref[...] = c
    attn.past(kc_ref[0, 0], vc_ref[0, 0], fo_ref[0, 0])

    @pl.when(j == pl.num_programs(1) - 1)
    def _():
        o_ref[0] = attn.finish(kn_ref[0], vn_ref[0], fn_ref[0] + c).astype(o_ref.dtype)


def _gather_attend(page_table, q, cache_kt, cache_vt, floc, k_new, v_new, floc_new):
    bd, n_pages = page_table.shape
    t, d = q.shape[1:]
    assert t == SAMPLE_SEQ
    npg = PAGES_PER_STEP
    nblk, kb = n_pages // npg, npg * PAGE
    pt = page_table.reshape(-1)

    def page_spec(shape, i):
        nd = len(shape) - 1
        return pl.BlockSpec(shape, lambda b, j, pt_ref: (pt_ref[b * n_pages + j * npg + i],) + (0,) * nd)

    per_b = lambda shape: pl.BlockSpec(shape, lambda b, j, pt_ref: (b, 0, 0))
    key_block = lambda rows: pl.BlockSpec((1, 1, rows, kb), lambda b, j, pt_ref: (b, j, 0, 0))
    kv_block = (1, N_HEADS, HEAD_DIM, PAGE)
    in_specs = ([per_b((1, t, d))]
                + [page_spec(kv_block, i) for i in range(npg)]
                + [page_spec(kv_block, i) for i in range(npg)]
                + [page_spec((1, N_HEADS, PAGE), i) for i in range(npg)]
                + [per_b((1, t, d)), per_b((1, t, d)), per_b((1, N_HEADS, PAGE))])
    grid_spec = pltpu.PrefetchScalarGridSpec(
        num_scalar_prefetch=1,
        grid=(bd, nblk),
        in_specs=in_specs,
        out_specs=[key_block(D_MODEL), key_block(D_MODEL), key_block(N_HEADS), per_b((1, t, d))],
        scratch_shapes=[pltpu.VMEM((N_HEADS, 1), _F32)] + _SampleAttention.SCRATCH,
    )
    return pl.pallas_call(
        _gather_attend_body,
        grid_spec=grid_spec,
        out_shape=[jax.ShapeDtypeStruct((bd, nblk, D_MODEL, kb), _BF16),
                   jax.ShapeDtypeStruct((bd, nblk, D_MODEL, kb), _BF16),
                   jax.ShapeDtypeStruct((bd, nblk, N_HEADS, kb), _F32),
                   jax.ShapeDtypeStruct((bd, t, d), _BF16)],
        compiler_params=_params(("arbitrary", "arbitrary")),
        name="gather_attend",
    )(pt, q, *([cache_kt] * npg), *([cache_vt] * npg), *([floc] * npg), k_new, v_new, floc_new)


def _paged_body(q_ref, k_ref, v_ref, f_ref, kn_ref, vn_ref, fn_ref, o_ref, *scratch):
    attn = _SampleAttention(*scratch)
    j = pl.program_id(1)

    @pl.when(j == 0)
    def _():
        attn.start(q_ref[0])

    for u in range(k_ref.shape[1]):
        attn.past(k_ref[0, u], v_ref[0, u], f_ref[0, u])

    @pl.when(j == pl.num_programs(1) - 1)
    def _():
        f_last = f_ref[0, f_ref.shape[1] - 1]
        f_new = fn_ref[0] + f_last[:, f_last.shape[1] - 1:]
        o_ref[0] = attn.finish(kn_ref[0], vn_ref[0], f_new).astype(o_ref.dtype)


def _paged(q, kc, vc, f_past, k_new, v_new, floc_new):
    bd, t, d = q.shape
    assert t == SAMPLE_SEQ
    nblk, kb = kc.shape[1], kc.shape[3]
    per_step = PAGED_KEYS // kb
    per_b = lambda shape: pl.BlockSpec(shape, lambda b, j: (b, 0, 0))
    key_blocks = lambda rows: pl.BlockSpec((1, per_step, rows, kb), lambda b, j: (b, j, 0, 0))
    in_specs = [per_b((1, t, d)), key_blocks(d), key_blocks(d), key_blocks(N_HEADS),
                per_b((1, t, d)), per_b((1, t, d)), per_b((1, N_HEADS, PAGE))]
    return pl.pallas_call(
        _paged_body,
        grid=(bd, nblk // per_step),
        in_specs=in_specs,
        out_specs=per_b((1, t, d)),
        out_shape=jax.ShapeDtypeStruct((bd, t, d), _BF16),
        scratch_shapes=_SampleAttention.SCRATCH,
        compiler_params=_params(("arbitrary", "arbitrary")),
        name="paged_sample",
    )(q, kc, vc, f_past, k_new, v_new, floc_new)


def _head_slots(w):
    w = w.reshape(D_MODEL, N_HEADS, HEAD_DIM)
    w = jnp.pad(w, ((0, 0), (0, 0), (0, HEAD_SLOT - HEAD_DIM)))
    return w.reshape(D_MODEL, N_HEADS * HEAD_SLOT)


def _constants():
    tri = np.tril(np.ones((ROW_TILE, ROW_TILE), np.float32))
    triu = np.triu(np.ones((PAGE, PAGE), np.float32))
    scat = np.zeros((LANES, N_HEADS * HEAD_SLOT), np.float32)
    ones = np.zeros((1, N_HEADS * HEAD_SLOT), np.float32)
    for part in range(3):
        for hd in range(N_HEADS):
            scat[part * N_HEADS + hd, hd * HEAD_SLOT + BIAS_LANE + part] = 1.0
            ones[0, hd * HEAD_SLOT + BIAS_LANE + part] = 1.0
    return (jnp.asarray(tri, _BF16), jnp.asarray(triu, _BF16), jnp.asarray(scat, _BF16),
            jnp.asarray(ones, _F32))


def kernel(x_prompt, x_sample, state_conv, cache_k, cache_v, cache_logf, page_table, p_prompt, p_sample,
           norm_g, w_in, w_conv, w_out, g_kv, w_k, w_v, w_f, b_f, w_q, w_o,
           w_gate, w_up, w_down, w_ple, w_ple_gate):
    b, s, d = x_prompt.shape
    bd, t, _ = x_sample.shape
    depth = norm_g.shape[0]
    bf = lambda w: w.astype(_BF16)
    tri, triu, scat, ones = _constants()

    w_k_b, w_v_b, w_kt_b, w_vt_b = bf(w_k), bf(w_v), bf(w_k.T), bf(w_v.T)
    w_ka = bf(_head_slots(w_k))
    w_q_scaled = w_q * SCALE
    w_qa = [bf(_head_slots(w_q_scaled[j] * LOG2E)) for j in range(w_q.shape[0])]
    w_qs = bf(w_q_scaled)
    w_f_pad = bf(jnp.pad(w_f, ((0, 0), (0, LANES - N_HEADS))))
    b_f_pad = jnp.pad(b_f, (0, LANES - N_HEADS)).reshape(1, LANES)
    gkv = g_kv.reshape(1, d)
    w_in_b, w_out_b, w_o_b = bf(w_in), bf(w_out), bf(w_o)
    ffn_w = (bf(w_gate), bf(w_up), bf(w_down), bf(w_ple), bf(w_ple_gate))
    pp = p_prompt.reshape(depth, b * s, PLE_DIM)
    ps = p_sample.reshape(depth, bd * t, PLE_DIM)

    hp = x_prompt
    conv_prompt = []
    for i in range(N_CONV_LAYERS):
        hp, st = _conv_prompt(hp, norm_g[i], w_in_b, w_conv[i], w_out_b, i)
        conv_prompt.append(st)
        hp = _ffn(hp.reshape(b * s, d), None, pp, i, norm_g[i], None, None, *ffn_w).reshape(b, s, d)
    kt_p, vt_p, lf_p, ka, vt, qa = _kvq_prompt(
        hp, gkv, norm_g[N_CONV_LAYERS, 0:1], w_kt_b, w_vt_b, w_f_pad, b_f_pad, w_ka, w_qa[0],
        tri, scat, ones)
    heads_last = lambda xt: jnp.transpose(xt.reshape(b, N_HEADS, HEAD_DIM, s), (0, 3, 1, 2))
    for i in range(N_CONV_LAYERS, depth):
        jl = i - N_CONV_LAYERS
        if jl > 0:
            qa = _q_prompt(hp, norm_g[i, 0:1], w_qa[jl], ones)
        o = _flash(qa, ka, vt)
        hp = _ffn(hp.reshape(b * s, d), o.reshape(b * s, d), pp, i, norm_g[i], w_o_b, jl,
                  *ffn_w).reshape(b, s, d)

    n = bd * t
    hs = x_sample.reshape(n, d)
    zeros_rows = jnp.zeros((bd, t - 2, d), _F32)
    conv_sample = []
    for i in range(N_CONV_LAYERS):
        st = state_conv[i]
        fix1 = jnp.concatenate([st[:, 1:2], st[:, 1:2], zeros_rows], axis=1).reshape(n, d)
        fix2 = jnp.concatenate([st, zeros_rows], axis=1).reshape(n, d)
        hs, cu = _conv_sample(hs, fix1, fix2, norm_g[i], w_in_b, w_conv[i], w_out_b, i)
        conv_sample.append(cu.reshape(bd, t, d)[:, t - 2:])
        hs = _ffn(hs, None, ps, i, norm_g[i], None, None, *ffn_w)
    k_s, v_s, lf_s, q_s = _kvq_sample(hs, gkv, norm_g[N_CONV_LAYERS, 0:1], w_k_b, w_v_b, w_f_pad,
                                      b_f_pad, w_qs[0])
    floc = _page_cumsum(jnp.swapaxes(cache_logf, 1, 2), triu, PREPASS_PAGES)
    lf_new = jnp.pad(jnp.swapaxes(lf_s.reshape(bd, t, N_HEADS), 1, 2), ((0, 0), (0, 0), (0, PAGE - t)))
    floc_new = _page_cumsum(lf_new, triu, bd)
    to_key_minor = lambda c: jnp.transpose(c, (0, 2, 3, 1))
    k_new = k_s.reshape(bd, t, d)
    v_new = v_s.reshape(bd, t, d)
    for i in range(N_CONV_LAYERS, depth):
        jl = i - N_CONV_LAYERS
        if jl == 0:
            kc, vc, f_past, o = _gather_attend(page_table, q_s.reshape(bd, t, d), to_key_minor(cache_k),
                                               to_key_minor(cache_v), floc, k_new, v_new, floc_new)
        else:
            q_s = _q_sample(hs, norm_g[i, 0:1], w_qs[jl])
            o = _paged(q_s.reshape(bd, t, d), kc, vc, f_past, k_new, v_new, floc_new)
        hs = _ffn(hs, o.reshape(n, d), ps, i, norm_g[i], w_o_b, jl, *ffn_w)

    return (hp, hs.reshape(bd, t, d), jnp.stack(conv_prompt, axis=0),
            heads_last(kt_p), heads_last(vt_p), lf_p,
            jnp.stack(conv_sample, axis=0),
            k_s.reshape(bd, t, N_HEADS, HEAD_DIM), v_s.reshape(bd, t, N_HEADS, HEAD_DIM),
            lf_s.reshape(bd, t, N_HEADS))
```

```python
import functools

import jax
import jax.numpy as jnp
import numpy as np
from jax import lax
from jax.experimental import pallas as pl
from jax.experimental.pallas import tpu as pltpu

D_MODEL = 1024
N_HEADS = 16
HEAD_DIM = 64
D_FF = 2816
PLE_DIM = 256
EPS = 1e-6
N_CONV_LAYERS = 2
PAGE = 128
LANES = 128
SUBLANES = 8
SAMPLE_SEQ = 8
HEAD_SLOT = 128
BIAS_LANE = HEAD_DIM
SCALE = HEAD_DIM ** -0.5
LOG2E = 1.4426950408889634
VT_ROWS = HEAD_DIM + 16

ROW_TILE = 512
FFN_TILE = 256
Q_TILE = 512
FLASH_Q_SPLIT = 256
HEAD_GROUP = 4
PAGES_PER_STEP = 8
PAGED_KEYS = 2048
PREPASS_PAGES = 128
VMEM_LIMIT = 56 * 1024 * 1024

_F32 = jnp.float32
_BF16 = jnp.bfloat16


def _rms(x, g):
    return x * lax.rsqrt(jnp.mean(x * x, axis=-1, keepdims=True) + EPS) * g


def _dot(a, b):
    return jnp.dot(a, b, preferred_element_type=_F32)


def _dot_nt(a, b):
    return lax.dot_general(a, b, (((1,), (1,)), ((), ())), preferred_element_type=_F32)


def _split3(x):
    hi = x.astype(_BF16)
    r = x - hi.astype(_F32)
    mid = r.astype(_BF16)
    lo = (r - mid.astype(_F32)).astype(_BF16)
    return hi, mid, lo


def _const_spec(shape):
    nd = len(shape)
    return pl.BlockSpec(shape, lambda *_: (0,) * nd, pipeline_mode=pl.Buffered(1))


def _layer_spec(shape, layer):
    nd = len(shape) - 1
    return pl.BlockSpec((None,) + tuple(shape[1:]), lambda *_: (layer,) + (0,) * nd,
                        pipeline_mode=pl.Buffered(1))


def _params(sem):
    return pltpu.CompilerParams(dimension_semantics=sem, vmem_limit_bytes=VMEM_LIMIT)


def _conv_body(sample, *refs):
    if sample:
        (h_ref, fix1_ref, fix2_ref, g_ref, win_ref, wconv_ref, wout_ref, out_ref, cu_ref) = refs
    else:
        (h_ref, g_ref, win_ref, wconv_ref, wout_ref, out_ref, state_ref, carry_ref) = refs

        @pl.when(pl.program_id(1) == 0)
        def _():
            carry_ref[...] = jnp.zeros_like(carry_ref)

    x = h_ref[...].reshape(h_ref.shape[-2:])
    tm = x.shape[0]
    xn = _rms(x, g_ref[0:1, :]).astype(_BF16)
    proj = _dot(xn, win_ref[...])
    bg = proj[:, :D_MODEL]
    cu = proj[:, D_MODEL:2 * D_MODEL] * proj[:, 2 * D_MODEL:]
    s1 = pltpu.roll(cu, 1, 0)
    s2 = pltpu.roll(cu, 2, 0)
    row = lax.broadcasted_iota(jnp.int32, (tm, 1), 0)
    if sample:
        pos = row % SAMPLE_SEQ
        s1 = jnp.where(pos == 0, fix1_ref[...], s1)
        s2 = jnp.where(pos < 2, fix2_ref[...], s2)
    else:
        prev2 = carry_ref[0:1, :]
        prev1 = carry_ref[1:2, :]
        s1 = jnp.where(row == 0, prev1, s1)
        s2 = jnp.where(row == 0, prev2, jnp.where(row == 1, prev1, s2))
    conv = wconv_ref[0:1, :] * s2 + wconv_ref[1:2, :] * s1 + wconv_ref[2:3, :] * cu
    y = _dot((bg * conv).astype(_BF16), wout_ref[...])
    out = x + _rms(y, g_ref[1:2, :])
    out_ref[...] = out.reshape(out_ref.shape)
    if sample:
        cu_ref[...] = cu
    else:
        tail = cu[tm - 2:tm, :]
        carry_ref[0:2, :] = tail
        state_ref[0] = tail


def _conv_prompt(h, g, w_in, w_conv, w_out, layer):
    b, s, d = h.shape
    tm = ROW_TILE
    return pl.pallas_call(
        functools.partial(_conv_body, False),
        grid=(b, s // tm),
        in_specs=[
            pl.BlockSpec((1, tm, d), lambda i, t: (i, t, 0)),
            _const_spec(g.shape), _layer_spec(w_in.shape, layer), _const_spec(w_conv.shape),
            _layer_spec(w_out.shape, layer),
        ],
        out_specs=[
            pl.BlockSpec((1, tm, d), lambda i, t: (i, t, 0)),
            pl.BlockSpec((1, 2, d), lambda i, t: (i, 0, 0)),
        ],
        out_shape=[jax.ShapeDtypeStruct(h.shape, _F32), jax.ShapeDtypeStruct((b, 2, d), _F32)],
        scratch_shapes=[pltpu.VMEM((SUBLANES, d), _F32)],
        compiler_params=_params(("arbitrary", "arbitrary")),
        name="conv_prompt",
    )(h, g, w_in, w_conv, w_out)


def _conv_sample(h, fix1, fix2, g, w_in, w_conv, w_out, layer):
    n, d = h.shape
    args = (h, fix1, fix2, g, w_in, w_conv, w_out)
    specs = [_const_spec(a.shape) for a in args]
    specs[4] = _layer_spec(w_in.shape, layer)
    specs[6] = _layer_spec(w_out.shape, layer)
    return pl.pallas_call(
        functools.partial(_conv_body, True),
        grid=(1,),
        in_specs=specs,
        out_specs=[_const_spec((n, d)), _const_spec((n, d))],
        out_shape=[jax.ShapeDtypeStruct((n, d), _F32), jax.ShapeDtypeStruct((n, d), _F32)],
        compiler_params=_params(("arbitrary",)),
        name="conv_sample",
    )(*args)


def _gather_pages(k_refs, v_refs, f_refs, kc_ref, vc_ref, fo_ref, carry_ref, first_block):
    c = jnp.where(first_block, 0.0, carry_ref[...])
    for i in range(PAGES_PER_STEP):
        keys = slice(i * PAGE, (i + 1) * PAGE)
        kc_ref[0, 0, :, keys] = k_refs[i][0].reshape(D_MODEL, PAGE).astype(_BF16)
        vc_ref[0, 0, :, keys] = v_refs[i][0].reshape(D_MODEL, PAGE).astype(_BF16)
        f = f_refs[i][0] + c
        fo_ref[0, 0, :, keys] = f
        c = f[:, PAGE - 1:PAGE]
    carry_ref[...] = c


def _ffn_body(attn, blocks_per_seq, *refs):
    npg = PAGES_PER_STEP
    gather = blocks_per_seq is not None
    if gather:
        nb = 10 if attn else 8
        refs = refs[1:]
        k_refs, v_refs, f_refs = (refs[nb + i * npg:nb + (i + 1) * npg] for i in range(3))
        out_ref, kc_ref, vc_ref, fo_ref, carry_ref = refs[nb + 3 * npg:]
        refs = refs[:nb] + (out_ref,)
    if attn:
        (h_ref, o_ref, p_ref, g_ref, wo_ref, wg_ref, wu_ref, wd_ref, wple_ref, wpg_ref, out_ref) = refs
    else:
        (h_ref, p_ref, g_ref, wg_ref, wu_ref, wd_ref, wple_ref, wpg_ref, out_ref) = refs
    x = h_ref[...]
    if attn:
        x = x + _rms(_dot(o_ref[...], wo_ref[...]), g_ref[1:2, :])
    xn = _rms(x, g_ref[2:3, :]).astype(_BF16)
    gate = _dot(xn, wg_ref[...])
    up = _dot(xn, wu_ref[...])
    act = (gate * jax.nn.sigmoid(gate) * up).astype(_BF16)
    x = x + _rms(_dot(act, wd_ref[...]), g_ref[3:4, :])
    pg = jax.nn.sigmoid(_dot(x.astype(_BF16), wpg_ref[...]))
    pe = _dot(p_ref[...].astype(_BF16), wple_ref[...])
    out_ref[...] = x + pg * pe
    if gather:
        _gather_pages(k_refs, v_refs, f_refs, kc_ref, vc_ref, fo_ref, carry_ref,
                      pl.program_id(0) % blocks_per_seq == 0)


def _ffn(h, o, p, layer, g, w_o, attn_layer, w_gate, w_up, w_down, w_ple, w_ple_gate, pages=None):
    n, d = h.shape
    tm = min(FFN_TILE, n)
    steps = n // tm
    row = lambda c: pl.BlockSpec((tm, c), lambda t, *_: (t, 0))
    p_spec = pl.BlockSpec((None, tm, p.shape[2]), lambda t, *_: (layer, t, 0))
    attn = o is not None
    args = [h] + ([o] if attn else []) + [p, g] + ([w_o] if attn else []) + [
        w_gate, w_up, w_down, w_ple, w_ple_gate]
    specs = [row(d)] + ([row(d)] if attn else []) + [p_spec, _const_spec(g.shape)] + (
        [_layer_spec(w_o.shape, attn_layer)] if attn else []) + [
        _layer_spec(w.shape, layer) for w in (w_gate, w_up, w_down, w_ple, w_ple_gate)]
    out_specs = [row(d)]
    out_shape = [jax.ShapeDtypeStruct((n, d), _F32)]
    scratch = []
    prefetch = []
    blocks_per_seq = None
    if pages is not None:
        page_table, cache_kt, cache_vt, floc = pages
        nseq, n_pages = page_table.shape
        npg = PAGES_PER_STEP
        blocks_per_seq, kb = n_pages // npg, npg * PAGE
        assert steps == nseq * blocks_per_seq

        def page_spec(shape, i):
            nd = len(shape) - 1
            return pl.BlockSpec(shape, lambda t, pt_ref: (pt_ref[t * npg + i],) + (0,) * nd)

        key_block = lambda rows: pl.BlockSpec(
            (1, 1, rows, kb), lambda t, pt_ref: (t // blocks_per_seq, t % blocks_per_seq, 0, 0))
        kv_block = (1, N_HEADS, HEAD_DIM, PAGE)
        prefetch = [page_table.reshape(-1)]
        args += [cache_kt] * npg + [cache_vt] * npg + [floc] * npg
        specs += ([page_spec(kv_block, i) for i in range(npg)] * 2
                  + [page_spec((1, N_HEADS, PAGE), i) for i in range(npg)])
        out_specs += [key_block(D_MODEL), key_block(D_MODEL), key_block(N_HEADS)]
        out_shape += [jax.ShapeDtypeStruct((nseq, blocks_per_seq, D_MODEL, kb), _BF16),
                      jax.ShapeDtypeStruct((nseq, blocks_per_seq, D_MODEL, kb), _BF16),
                      jax.ShapeDtypeStruct((nseq, blocks_per_seq, N_HEADS, kb), _F32)]
        scratch = [pltpu.VMEM((N_HEADS, 1), _F32)]
    grid_spec = pltpu.PrefetchScalarGridSpec(
        num_scalar_prefetch=len(prefetch), grid=(steps,), in_specs=specs, out_specs=out_specs,
        scratch_shapes=scratch)
    outs = pl.pallas_call(
        functools.partial(_ffn_body, attn, blocks_per_seq),
        grid_spec=grid_spec,
        out_shape=out_shape,
        compiler_params=_params(("arbitrary",)),
        name=("ffn_attn" if attn else "ffn") + ("_gather" if pages is not None else ""),
    )(*prefetch, *args)
    return outs[0] if pages is None else outs


def _log_sigmoid(x):
    return jnp.minimum(x, 0.0) - jnp.log1p(jnp.exp(-jnp.abs(x)))


def _store_heads(ref, x):
    for hd in range(N_HEADS):
        ref[0, hd] = x[:, hd * HEAD_SLOT:(hd + 1) * HEAD_SLOT].astype(ref.dtype)


def _kvq_prompt_body(h_ref, gkv_ref, gq_ref, wkt_ref, wvt_ref, wf_ref, bf_ref, wka_ref,
                     wqa_ref, tri_ref, scat_ref, ones_ref,
                     kt_ref, vto_ref, lf_ref, ka_ref, vt_ref, qa_ref, carry_ref):
    @pl.when(pl.program_id(1) == 0)
    def _():
        carry_ref[...] = jnp.zeros_like(carry_ref)

    x = h_ref[0]
    tm = x.shape[0]
    z = _rms(x, gkv_ref[...]).astype(_BF16)
    kt_ref[0] = _dot_nt(wkt_ref[...], z)
    vt = _dot_nt(wvt_ref[...], z)
    vto_ref[0] = vt
    lf = _log_sigmoid(_dot(z, wf_ref[...]) + bf_ref[...])
    lane = lax.broadcasted_iota(jnp.int32, (1, LANES), 1)
    lf = jnp.where(lane < N_HEADS, lf, 0.0)
    lf_ref[0] = lf[:, :N_HEADS]
    hi, mid, lo = _split3(lf)
    tri = tri_ref[...]
    f = _dot(tri, hi) + _dot(tri, mid) + _dot(tri, lo) + carry_ref[0:1, :]
    carry_ref[0:1, :] = f[tm - 1:tm, :]
    nhi, nmid, nlo = _split3(f * -LOG2E)
    parts = (nhi.astype(_F32) + pltpu.roll(nmid.astype(_F32), N_HEADS, 1)
             + pltpu.roll(nlo.astype(_F32), 2 * N_HEADS, 1)).astype(_BF16)
    ka = _dot(z, wka_ref[...]) + _dot(parts, scat_ref[...])
    _store_heads(ka_ref, ka)
    vt_ref[0, :, 0, :HEAD_DIM, :] = vt.astype(_BF16).reshape(N_HEADS, HEAD_DIM, tm)
    extra = lax.broadcasted_iota(jnp.int32, (N_HEADS, VT_ROWS - HEAD_DIM, tm), 1) == 0
    vt_ref[0, :, 0, HEAD_DIM:, :] = extra.astype(_F32).astype(_BF16)
    xq = _rms(x, gq_ref[...]).astype(_BF16)
    _store_heads(qa_ref, _dot(xq, wqa_ref[...]) + ones_ref[...])


def _q_prompt_body(h_ref, gq_ref, wqa_ref, ones_ref, qa_ref):
    xq = _rms(h_ref[0], gq_ref[...]).astype(_BF16)
    _store_heads(qa_ref, _dot(xq, wqa_ref[...]) + ones_ref[...])


def _head_major_spec(tm):
    return pl.BlockSpec((1, N_HEADS, tm, HEAD_SLOT), lambda i, t: (i, 0, t, 0))


def _kvq_prompt(h, gkv, gq, w_kt, w_vt, w_f, b_f, w_ka, w_qa, tri, scat, ones):
    b, s, d = h.shape
    tm = ROW_TILE
    nk = s // tm
    consts = (gkv, gq, w_kt, w_vt, w_f, b_f, w_ka, w_qa, tri, scat, ones)
    tile = lambda c: pl.BlockSpec((1, tm, c), lambda i, t: (i, t, 0))
    tile_t = pl.BlockSpec((1, d, tm), lambda i, t: (i, 0, t))
    hm = jax.ShapeDtypeStruct((b, N_HEADS, s, HEAD_SLOT), _BF16)
    return pl.pallas_call(
        _kvq_prompt_body,
        grid=(b, nk),
        in_specs=[tile(d)] + [_const_spec(c.shape) for c in consts],
        out_specs=[tile_t, tile_t, tile(N_HEADS), _head_major_spec(tm),
                   pl.BlockSpec((1, N_HEADS, 1, VT_ROWS, tm), lambda i, t: (i, 0, t, 0, 0)),
                   _head_major_spec(tm)],
        out_shape=[jax.ShapeDtypeStruct((b, d, s), _F32), jax.ShapeDtypeStruct((b, d, s), _F32),
                   jax.ShapeDtypeStruct((b, s, N_HEADS), _F32), hm,
                   jax.ShapeDtypeStruct((b, N_HEADS, nk, VT_ROWS, tm), _BF16), hm],
        scratch_shapes=[pltpu.VMEM((SUBLANES, LANES), _F32)],
        compiler_params=_params(("arbitrary", "arbitrary")),
        name="kvq_prompt",
    )(h, *consts)


def _q_prompt(h, gq, w_qa, ones):
    b, s, d = h.shape
    tm = ROW_TILE
    consts = (gq, w_qa, ones)
    return pl.pallas_call(
        _q_prompt_body,
        grid=(b, s // tm),
        in_specs=[pl.BlockSpec((1, tm, d), lambda i, t: (i, t, 0))] + [_const_spec(c.shape) for c in consts],
        out_specs=_head_major_spec(tm),
        out_shape=jax.ShapeDtypeStruct((b, N_HEADS, s, HEAD_SLOT), _BF16),
        compiler_params=_params(("arbitrary", "arbitrary")),
        name="q_prompt",
    )(h, *consts)


def _kvq_sample_body(kv, *refs):
    if kv:
        (h_ref, gkv_ref, gq_ref, wk_ref, wv_ref, wf_ref, bf_ref, wq_ref,
         k_ref, v_ref, lf_ref, q_ref) = refs
        x = h_ref[...]
        z = _rms(x, gkv_ref[...]).astype(_BF16)
        k_ref[...] = _dot(z, wk_ref[...])
        v_ref[...] = _dot(z, wv_ref[...])
        lf_ref[...] = _log_sigmoid(_dot(z, wf_ref[...]) + bf_ref[...])[:, :N_HEADS]
    else:
        h_ref, gq_ref, wq_ref, q_ref = refs
        x = h_ref[...]
    q_ref[...] = _dot(_rms(x, gq_ref[...]).astype(_BF16), wq_ref[...])


def _kvq_sample(h, gkv, gq, w_k, w_v, w_f, b_f, w_q):
    n, d = h.shape
    args = (h, gkv, gq, w_k, w_v, w_f, b_f, w_q)
    outs = [(n, d), (n, d), (n, N_HEADS), (n, d)]
    return pl.pallas_call(
        functools.partial(_kvq_sample_body, True),
        grid=(1,),
        in_specs=[_const_spec(a.shape) for a in args],
        out_specs=[_const_spec(o) for o in outs],
        out_shape=[jax.ShapeDtypeStruct(o, _F32) for o in outs],
        compiler_params=_params(("arbitrary",)),
        name="kvq_sample",
    )(*args)


def _q_sample(h, gq, w_q):
    n, d = h.shape
    args = (h, gq, w_q)
    return pl.pallas_call(
        functools.partial(_kvq_sample_body, False),
        grid=(1,),
        in_specs=[_const_spec(a.shape) for a in args],
        out_specs=_const_spec((n, d)),
        out_shape=jax.ShapeDtypeStruct((n, d), _F32),
        compiler_params=_params(("arbitrary",)),
        name="q_sample",
    )(*args)


def _flash_body(q_ref, k_ref, vt_ref, o_ref, s0_ref, s1_ref, m_ref, acc_ref):
    qi = pl.program_id(2)
    tq = q_ref.shape[2]
    tk = vt_ref.shape[4]
    slots = (s0_ref, s1_ref)
    halves = [slice(c, c + FLASH_Q_SPLIT) for c in range(0, tq, FLASH_Q_SPLIT)]

    def scores(hd, j, slot):
        kb = k_ref[0, hd, pl.ds(pl.multiple_of(j * tk, tk), tk), :]
        for cols in halves:
            slots[slot][hd, :, cols] = _dot_nt(kb, q_ref[0, hd, cols, :])

    def update(hd, j, slot, masked):
        for cols in halves:
            keys = slice(0, min(cols.stop, tk)) if masked else slice(0, tk)
            st = slots[slot][hd, keys, cols]
            if masked:
                key_pos = lax.broadcasted_iota(jnp.int32, st.shape, 0)
                qry_pos = lax.broadcasted_iota(jnp.int32, st.shape, 1) + cols.start
                st = jnp.where(key_pos <= qry_pos, st, -jnp.inf)
            m = m_ref[hd, :, cols]
            m_new = jnp.maximum(m, jnp.max(st, axis=0, keepdims=True))
            p = jnp.exp2(st - m_new).astype(_BF16)
            acc_ref[hd, :, cols] = (jnp.exp2(m - m_new) * acc_ref[hd, :, cols]
                                    + _dot(vt_ref[0, hd, j, :, keys], p))
            m_ref[hd, :, cols] = m_new

    m_ref[...] = jnp.full_like(m_ref, -jnp.inf)
    acc_ref[...] = jnp.zeros_like(acc_ref)
    for hd in range(HEAD_GROUP):
        scores(hd, 0, 0)

    def pair(jj, carry):
        j = 2 * jj
        for hd in range(HEAD_GROUP):
            scores(hd, j + 1, 1)
            update(hd, j, 0, False)
        for hd in range(HEAD_GROUP):
            scores(hd, j + 2, 0)
            update(hd, j + 1, 1, False)
        return carry

    lax.fori_loop(0, qi // 2, pair, 0)

    @pl.when(qi % 2 == 0)
    def _():
        for hd in range(HEAD_GROUP):
            update(hd, qi, 0, True)

    @pl.when(qi % 2 == 1)
    def _():
        for hd in range(HEAD_GROUP):
            scores(hd, qi, 1)
            update(hd, qi - 1, 0, False)
        for hd in range(HEAD_GROUP):
            update(hd, qi, 1, True)

    outs = [acc_ref[hd, :HEAD_DIM, :] * (1.0 / acc_ref[hd, HEAD_DIM:HEAD_DIM + 1, :])
            for hd in range(HEAD_GROUP)]
    o_ref[0] = jnp.concatenate(outs, axis=0).T.astype(o_ref.dtype)


def _flash(qa, ka, vt):
    b, _, s, _ = qa.shape
    nk, tk = vt.shape[2], vt.shape[4]
    tq = Q_TILE
    assert tq == tk
    hg = HEAD_GROUP
    return pl.pallas_call(
        _flash_body,
        grid=(b, N_HEADS // hg, s // tq),
        in_specs=[
            pl.BlockSpec((1, hg, tq, HEAD_SLOT), lambda i, g, t: (i, g, t, 0)),
            pl.BlockSpec((1, hg, s, HEAD_SLOT), lambda i, g, t: (i, g, 0, 0)),
            pl.BlockSpec((1, hg, nk, VT_ROWS, tk), lambda i, g, t: (i, g, 0, 0, 0)),
        ],
        out_specs=pl.BlockSpec((1, tq, hg * HEAD_DIM), lambda i, g, t: (i, t, g)),
        out_shape=jax.ShapeDtypeStruct((b, s, D_MODEL), _BF16),
        scratch_shapes=[pltpu.VMEM((hg, tk, tq), _F32), pltpu.VMEM((hg, tk, tq), _F32),
                        pltpu.VMEM((hg, 1, tq), _F32), pltpu.VMEM((hg, VT_ROWS, tq), _F32)],
        compiler_params=_params(("arbitrary", "arbitrary", "arbitrary")),
        name="flash_prompt",
    )(qa, ka, vt)


def _page_cumsum_body(x_ref, triu_ref, o_ref):
    x = x_ref[...].reshape(-1, PAGE)
    hi, mid, lo = _split3(x)
    tri = triu_ref[...]
    o_ref[...] = (_dot(hi, tri) + _dot(mid, tri) + _dot(lo, tri)).reshape(o_ref.shape)


def _page_cumsum(xt, triu, pages):
    n = xt.shape[0]
    spec = pl.BlockSpec((pages, N_HEADS, PAGE), lambda i: (i, 0, 0))
    return pl.pallas_call(
        _page_cumsum_body,
        grid=(n // pages,),
        in_specs=[spec, _const_spec(triu.shape)],
        out_specs=spec,
        out_shape=jax.ShapeDtypeStruct(xt.shape, _F32),
        compiler_params=_params(("arbitrary",)),
        name="page_cumsum",
    )(xt, triu)


def _expand_heads(x):
    n = x.shape[1]
    return jnp.concatenate(
        [jnp.broadcast_to(x[hd:hd + 1, :], (SAMPLE_SEQ, n)) for hd in range(N_HEADS)], axis=0)


class _SampleAttention:
    ROWS = N_HEADS * SAMPLE_SEQ
    SCRATCH = [pltpu.VMEM((ROWS, D_MODEL), _BF16), pltpu.VMEM((ROWS, 1), _F32),
               pltpu.VMEM((ROWS, 1), _F32), pltpu.VMEM((ROWS, D_MODEL), _F32)]

    def __init__(self, qbd_ref, m_ref, l_ref, acc_ref):
        self.qbd_ref, self.m_ref, self.l_ref, self.acc_ref = qbd_ref, m_ref, l_ref, acc_ref
        head_of_row = lax.broadcasted_iota(jnp.int32, (self.ROWS, D_MODEL), 0) // SAMPLE_SEQ
        head_of_col = lax.broadcasted_iota(jnp.int32, (self.ROWS, D_MODEL), 1) // HEAD_DIM
        self.blockdiag = head_of_row == head_of_col

    def start(self, q):
        self.qbd_ref[...] = jnp.where(self.blockdiag, jnp.concatenate([q] * N_HEADS, axis=0),
                                      0.0).astype(_BF16)
        self.m_ref[...] = jnp.full_like(self.m_ref, -jnp.inf)
        self.l_ref[...] = jnp.zeros_like(self.l_ref)
        self.acc_ref[...] = jnp.zeros_like(self.acc_ref)

    def _attend(self, st, v, v_is_transposed):
        m = self.m_ref[...]
        m_new = jnp.maximum(m, jnp.max(st, axis=1, keepdims=True))
        alpha = jnp.exp(m - m_new)
        p = jnp.exp(st - m_new)
        self.l_ref[...] = alpha * self.l_ref[...] + jnp.sum(p, axis=1, keepdims=True)
        pv = _dot_nt(p.astype(_BF16), v) if v_is_transposed else _dot(p.astype(_BF16), v)
        self.acc_ref[...] = alpha * self.acc_ref[...] + pv
        self.m_ref[...] = m_new

    def past(self, kt, vt, f):
        self._attend(_dot(self.qbd_ref[...], kt) - _expand_heads(f), vt, True)

    def finish(self, k_new, v_new, f_new):
        pad = jnp.zeros((PAGE - SAMPLE_SEQ, D_MODEL), _F32)
        kn = jnp.concatenate([k_new, pad], axis=0).astype(_BF16)
        vn = jnp.concatenate([v_new, pad], axis=0).astype(_BF16)
        st = _dot_nt(self.qbd_ref[...], kn) - _expand_heads(f_new)
        key = lax.broadcasted_iota(jnp.int32, (self.ROWS, PAGE), 1)
        qpos = lax.broadcasted_iota(jnp.int32, (self.ROWS, PAGE), 0) % SAMPLE_SEQ
        self._attend(jnp.where(key <= qpos, st, -jnp.inf), vn, False)
        full = jnp.where(self.blockdiag, self.acc_ref[...] * (1.0 / self.l_ref[...]), 0.0)
        out = full[0:SAMPLE_SEQ, :]
        for hd in range(1, N_HEADS):
            out = out + full[hd * SAMPLE_SEQ:(hd + 1) * SAMPLE_SEQ, :]
        return out


def _paged_body(q_ref, k_ref, v_ref, f_ref, kn_ref, vn_ref, fn_ref, o_ref, *scratch):
    attn = _SampleAttention(*scratch)
    j = pl.program_id(1)

    @pl.when(j == 0)
    def _():
        attn.start(q_ref[0])

    for u in range(k_ref.shape[1]):
        attn.past(k_ref[0, u], v_ref[0, u], f_ref[0, u])

    @pl.when(j == pl.num_programs(1) - 1)
    def _():
        f_last = f_ref[0, f_ref.shape[1] - 1]
        f_new = fn_ref[0] + f_last[:, f_last.shape[1] - 1:]
        o_ref[0] = attn.finish(kn_ref[0], vn_ref[0], f_new).astype(o_ref.dtype)


def _paged(q, kc, vc, f_past, k_new, v_new, floc_new):
    bd, t, d = q.shape
    assert t == SAMPLE_SEQ
    nblk, kb = kc.shape[1], kc.shape[3]
    per_step = PAGED_KEYS // kb
    per_b = lambda shape: pl.BlockSpec(shape, lambda b, j: (b, 0, 0))
    key_blocks = lambda rows: pl.BlockSpec((1, per_step, rows, kb), lambda b, j: (b, j, 0, 0))
    in_specs = [per_b((1, t, d)), key_blocks(d), key_blocks(d), key_blocks(N_HEADS),
                per_b((1, t, d)), per_b((1, t, d)), per_b((1, N_HEADS, PAGE))]
    return pl.pallas_call(
        _paged_body,
        grid=(bd, nblk // per_step),
        in_specs=in_specs,
        out_specs=per_b((1, t, d)),
        out_shape=jax.ShapeDtypeStruct((bd, t, d), _BF16),
        scratch_shapes=_SampleAttention.SCRATCH,
        compiler_params=_params(("arbitrary", "arbitrary")),
        name="paged_sample",
    )(q, kc, vc, f_past, k_new, v_new, floc_new)


def _head_slots(w):
    w = w.reshape(D_MODEL, N_HEADS, HEAD_DIM)
    w = jnp.pad(w, ((0, 0), (0, 0), (0, HEAD_SLOT - HEAD_DIM)))
    return w.reshape(D_MODEL, N_HEADS * HEAD_SLOT)


def _constants():
    tri = np.tril(np.ones((ROW_TILE, ROW_TILE), np.float32))
    triu = np.triu(np.ones((PAGE, PAGE), np.float32))
    scat = np.zeros((LANES, N_HEADS * HEAD_SLOT), np.float32)
    ones = np.zeros((1, N_HEADS * HEAD_SLOT), np.float32)
    for part in range(3):
        for hd in range(N_HEADS):
            scat[part * N_HEADS + hd, hd * HEAD_SLOT + BIAS_LANE + part] = 1.0
            ones[0, hd * HEAD_SLOT + BIAS_LANE + part] = 1.0
    return (jnp.asarray(tri, _BF16), jnp.asarray(triu, _BF16), jnp.asarray(scat, _BF16),
            jnp.asarray(ones, _F32))


def kernel(x_prompt, x_sample, state_conv, cache_k, cache_v, cache_logf, page_table, p_prompt, p_sample,
           norm_g, w_in, w_conv, w_out, g_kv, w_k, w_v, w_f, b_f, w_q, w_o,
           w_gate, w_up, w_down, w_ple, w_ple_gate):
    b, s, d = x_prompt.shape
    bd, t, _ = x_sample.shape
    depth = norm_g.shape[0]
    bf = lambda w: w.astype(_BF16)
    tri, triu, scat, ones = _constants()

    w_k_b, w_v_b, w_kt_b, w_vt_b = bf(w_k), bf(w_v), bf(w_k.T), bf(w_v.T)
    w_ka = bf(_head_slots(w_k))
    w_q_scaled = w_q * SCALE
    w_qa = [bf(_head_slots(w_q_scaled[j] * LOG2E)) for j in range(w_q.shape[0])]
    w_qs = bf(w_q_scaled)
    w_f_pad = bf(jnp.pad(w_f, ((0, 0), (0, LANES - N_HEADS))))
    b_f_pad = jnp.pad(b_f, (0, LANES - N_HEADS)).reshape(1, LANES)
    gkv = g_kv.reshape(1, d)
    w_in_b, w_out_b, w_o_b = bf(w_in), bf(w_out), bf(w_o)
    ffn_w = (bf(w_gate), bf(w_up), bf(w_down), bf(w_ple), bf(w_ple_gate))
    pp = p_prompt.reshape(depth, b * s, PLE_DIM)
    ps = p_sample.reshape(depth, bd * t, PLE_DIM)

    floc = _page_cumsum(jnp.swapaxes(cache_logf, 1, 2), triu, PREPASS_PAGES)
    to_key_minor = lambda c: jnp.transpose(c, (0, 2, 3, 1))
    cache_kt, cache_vt = to_key_minor(cache_k), to_key_minor(cache_v)
    seq_groups = [slice(g * bd // depth, (g + 1) * bd // depth) for g in range(depth)]
    dense = []

    def prompt_ffn(hp, o, i, w_o_stack, jl):
        hp, kc, vc, f_past = _ffn(hp.reshape(b * s, d), o, pp, i, norm_g[i], w_o_stack, jl, *ffn_w,
                                  pages=(page_table[seq_groups[i]], cache_kt, cache_vt, floc))
        dense.append((kc, vc, f_past))
        return hp.reshape(b, s, d)

    hp = x_prompt
    conv_prompt = []
    for i in range(N_CONV_LAYERS):
        hp, st = _conv_prompt(hp, norm_g[i], w_in_b, w_conv[i], w_out_b, i)
        conv_prompt.append(st)
        hp = prompt_ffn(hp, None, i, None, None)
    kt_p, vt_p, lf_p, ka, vt, qa = _kvq_prompt(
        hp, gkv, norm_g[N_CONV_LAYERS, 0:1], w_kt_b, w_vt_b, w_f_pad, b_f_pad, w_ka, w_qa[0],
        tri, scat, ones)
    heads_last = lambda xt: jnp.transpose(xt.reshape(b, N_HEADS, HEAD_DIM, s), (0, 3, 1, 2))
    for i in range(N_CONV_LAYERS, depth):
        jl = i - N_CONV_LAYERS
        if jl > 0:
            qa = _q_prompt(hp, norm_g[i, 0:1], w_qa[jl], ones)
        o = _flash(qa, ka, vt)
        hp = prompt_ffn(hp, o.reshape(b * s, d), i, w_o_b, jl)

    n = bd * t
    hs = x_sample.reshape(n, d)
    zeros_rows = jnp.zeros((bd, t - 2, d), _F32)
    conv_sample = []
    for i in range(N_CONV_LAYERS):
        st = state_conv[i]
        fix1 = jnp.concatenate([st[:, 1:2], st[:, 1:2], zeros_rows], axis=1).reshape(n, d)
        fix2 = jnp.concatenate([st, zeros_rows], axis=1).reshape(n, d)
        hs, cu = _conv_sample(hs, fix1, fix2, norm_g[i], w_in_b, w_conv[i], w_out_b, i)
        conv_sample.append(cu.reshape(bd, t, d)[:, t - 2:])
        hs = _ffn(hs, None, ps, i, norm_g[i], None, None, *ffn_w)
    k_s, v_s, lf_s, q_s = _kvq_sample(hs, gkv, norm_g[N_CONV_LAYERS, 0:1], w_k_b, w_v_b, w_f_pad,
                                      b_f_pad, w_qs[0])
    lf_new = jnp.pad(jnp.swapaxes(lf_s.reshape(bd, t, N_HEADS), 1, 2), ((0, 0), (0, 0), (0, PAGE - t)))
    floc_new = _page_cumsum(lf_new, triu, bd)
    k_new = k_s.reshape(bd, t, d)
    v_new = v_s.reshape(bd, t, d)
    for i in range(N_CONV_LAYERS, depth):
        jl = i - N_CONV_LAYERS
        if jl > 0:
            q_s = _q_sample(hs, norm_g[i, 0:1], w_qs[jl])
        q3 = q_s.reshape(bd, t, d)
        o = jnp.concatenate([_paged(q3[grp], *dense[g], k_new[grp], v_new[grp], floc_new[grp])
                             for g, grp in enumerate(seq_groups)], axis=0)
        hs = _ffn(hs, o.reshape(n, d), ps, i, norm_g[i], w_o_b, jl, *ffn_w)

    return (hp, hs.reshape(bd, t, d), jnp.stack(conv_prompt, axis=0),
            heads_last(kt_p), heads_last(vt_p), lf_p,
            jnp.stack(conv_sample, axis=0),
            k_s.reshape(bd, t, N_HEADS, HEAD_DIM), v_s.reshape(bd, t, N_HEADS, HEAD_DIM),
            lf_s.reshape(bd, t, N_HEADS))
```

```python
import functools

import jax
import jax.numpy as jnp
import numpy as np
from jax import lax
from jax.experimental import pallas as pl
from jax.experimental.pallas import tpu as pltpu

D_MODEL = 1024
N_HEADS = 16
HEAD_DIM = 64
D_FF = 2816
PLE_DIM = 256
EPS = 1e-6
N_CONV_LAYERS = 2
PAGE = 128
LANES = 128
MXU_DEPTH = 256
SUBLANES = 8
SAMPLE_SEQ = 8
HEAD_SLOT = 128
BIAS_LANE = HEAD_DIM
SCALE = HEAD_DIM ** -0.5
LOG2E = 1.4426950408889634
VT_ROWS = HEAD_DIM + 16

ROW_TILE = 512
FFN_TILE = 256
Q_TILE = 512
FLASH_Q_SPLIT = 256
HEAD_GROUP = 4
PAGES_PER_STEP = 8
PAGED_KEYS = 4096
PREPASS_PAGES = 128
VMEM_LIMIT = 56 * 1024 * 1024

_F32 = jnp.float32
_BF16 = jnp.bfloat16


def _rms(x, g):
    return x * lax.rsqrt(jnp.mean(x * x, axis=-1, keepdims=True) + EPS) * g


def _dot(a, b):
    return jnp.dot(a, b, preferred_element_type=_F32)


def _dot_nt(a, b):
    return lax.dot_general(a, b, (((1,), (1,)), ((), ())), preferred_element_type=_F32)


def _split3(x):
    hi = x.astype(_BF16)
    r = x - hi.astype(_F32)
    mid = r.astype(_BF16)
    lo = (r - mid.astype(_F32)).astype(_BF16)
    return hi, mid, lo


def _const_spec(shape):
    nd = len(shape)
    return pl.BlockSpec(shape, lambda *_: (0,) * nd, pipeline_mode=pl.Buffered(1))


def _layer_spec(shape, layer):
    nd = len(shape) - 1
    return pl.BlockSpec((None,) + tuple(shape[1:]), lambda *_: (layer,) + (0,) * nd,
                        pipeline_mode=pl.Buffered(1))


def _params(sem):
    return pltpu.CompilerParams(dimension_semantics=sem, vmem_limit_bytes=VMEM_LIMIT)


def _conv_body(sample, *refs):
    if sample:
        (h_ref, fix1_ref, fix2_ref, g_ref, win_ref, wconv_ref, wout_ref, out_ref, cu_ref) = refs
    else:
        (h_ref, g_ref, win_ref, wconv_ref, wout_ref, out_ref, state_ref, carry_ref) = refs

        @pl.when(pl.program_id(1) == 0)
        def _():
            carry_ref[...] = jnp.zeros_like(carry_ref)

    x = h_ref[...].reshape(h_ref.shape[-2:])
    tm = x.shape[0]
    xn = _rms(x, g_ref[0:1, :]).astype(_BF16)
    proj = _dot(xn, win_ref[...])
    bg = proj[:, :D_MODEL]
    cu = proj[:, D_MODEL:2 * D_MODEL] * proj[:, 2 * D_MODEL:]
    s1 = pltpu.roll(cu, 1, 0)
    s2 = pltpu.roll(cu, 2, 0)
    row = lax.broadcasted_iota(jnp.int32, (tm, 1), 0)
    if sample:
        pos = row % SAMPLE_SEQ
        s1 = jnp.where(pos == 0, fix1_ref[...], s1)
        s2 = jnp.where(pos < 2, fix2_ref[...], s2)
    else:
        prev2 = carry_ref[0:1, :]
        prev1 = carry_ref[1:2, :]
        s1 = jnp.where(row == 0, prev1, s1)
        s2 = jnp.where(row == 0, prev2, jnp.where(row == 1, prev1, s2))
    conv = wconv_ref[0:1, :] * s2 + wconv_ref[1:2, :] * s1 + wconv_ref[2:3, :] * cu
    y = _dot((bg * conv).astype(_BF16), wout_ref[...])
    out = x + _rms(y, g_ref[1:2, :])
    out_ref[...] = out.reshape(out_ref.shape)
    if sample:
        cu_ref[...] = cu
    else:
        tail = cu[tm - 2:tm, :]
        carry_ref[0:2, :] = tail
        state_ref[0] = tail


def _conv_prompt(h, g, w_in, w_conv, w_out, layer):
    b, s, d = h.shape
    tm = ROW_TILE
    return pl.pallas_call(
        functools.partial(_conv_body, False),
        grid=(b, s // tm),
        in_specs=[
            pl.BlockSpec((1, tm, d), lambda i, t: (i, t, 0)),
            _const_spec(g.shape), _layer_spec(w_in.shape, layer), _const_spec(w_conv.shape),
            _layer_spec(w_out.shape, layer),
        ],
        out_specs=[
            pl.BlockSpec((1, tm, d), lambda i, t: (i, t, 0)),
            pl.BlockSpec((1, 2, d), lambda i, t: (i, 0, 0)),
        ],
        out_shape=[jax.ShapeDtypeStruct(h.shape, _F32), jax.ShapeDtypeStruct((b, 2, d), _F32)],
        scratch_shapes=[pltpu.VMEM((SUBLANES, d), _F32)],
        compiler_params=_params(("arbitrary", "arbitrary")),
        name="conv_prompt",
    )(h, g, w_in, w_conv, w_out)


def _conv_sample(h, fix1, fix2, g, w_in, w_conv, w_out, layer):
    n, d = h.shape
    args = (h, fix1, fix2, g, w_in, w_conv, w_out)
    specs = [_const_spec(a.shape) for a in args]
    specs[4] = _layer_spec(w_in.shape, layer)
    specs[6] = _layer_spec(w_out.shape, layer)
    return pl.pallas_call(
        functools.partial(_conv_body, True),
        grid=(1,),
        in_specs=specs,
        out_specs=[_const_spec((n, d)), _const_spec((n, d))],
        out_shape=[jax.ShapeDtypeStruct((n, d), _F32), jax.ShapeDtypeStruct((n, d), _F32)],
        compiler_params=_params(("arbitrary",)),
        name="conv_sample",
    )(*args)


def _gather_pages(k_refs, v_refs, f_refs, kc_ref, vc_ref, fo_ref, carry_ref, first_block):
    c = jnp.where(first_block, 0.0, carry_ref[...])
    for i in range(PAGES_PER_STEP):
        keys = slice(i * PAGE, (i + 1) * PAGE)
        kc_ref[0, 0, :, keys] = k_refs[i][0].reshape(D_MODEL, PAGE).astype(_BF16)
        vc_ref[0, 0, keys, :] = v_refs[i][0].reshape(D_MODEL, PAGE).T.astype(_BF16)
        f = f_refs[i][0] + c
        fo_ref[0, 0, :, keys] = f
        c = f[:, PAGE - 1:PAGE]
    carry_ref[...] = c


def _ffn_body(attn, blocks_per_seq, *refs):
    npg = PAGES_PER_STEP
    gather = blocks_per_seq is not None
    if gather:
        nb = 10 if attn else 8
        refs = refs[1:]
        k_refs, v_refs, f_refs = (refs[nb + i * npg:nb + (i + 1) * npg] for i in range(3))
        out_ref, kc_ref, vc_ref, fo_ref, carry_ref = refs[nb + 3 * npg:]
        refs = refs[:nb] + (out_ref,)
    if attn:
        (h_ref, o_ref, p_ref, g_ref, wo_ref, wg_ref, wu_ref, wd_ref, wple_ref, wpg_ref, out_ref) = refs
    else:
        (h_ref, p_ref, g_ref, wg_ref, wu_ref, wd_ref, wple_ref, wpg_ref, out_ref) = refs
    x = h_ref[...]
    if attn:
        x = x + _rms(_dot(o_ref[...], wo_ref[...]), g_ref[1:2, :])
    xn = _rms(x, g_ref[2:3, :]).astype(_BF16)
    gate = _dot(xn, wg_ref[...])
    up = _dot(xn, wu_ref[...])
    act = (gate * jax.nn.sigmoid(gate) * up).astype(_BF16)
    x = x + _rms(_dot(act, wd_ref[...]), g_ref[3:4, :])
    pg = jax.nn.sigmoid(_dot(x.astype(_BF16), wpg_ref[...]))
    pe = _dot(p_ref[...].astype(_BF16), wple_ref[...])
    out_ref[...] = x + pg * pe
    if gather:
        _gather_pages(k_refs, v_refs, f_refs, kc_ref, vc_ref, fo_ref, carry_ref,
                      pl.program_id(0) % blocks_per_seq == 0)


def _ffn(h, o, p, layer, g, w_o, attn_layer, w_gate, w_up, w_down, w_ple, w_ple_gate, pages=None):
    n, d = h.shape
    tm = min(FFN_TILE, n)
    steps = n // tm
    row = lambda c: pl.BlockSpec((tm, c), lambda t, *_: (t, 0))
    p_spec = pl.BlockSpec((None, tm, p.shape[2]), lambda t, *_: (layer, t, 0))
    attn = o is not None
    args = [h] + ([o] if attn else []) + [p, g] + ([w_o] if attn else []) + [
        w_gate, w_up, w_down, w_ple, w_ple_gate]
    specs = [row(d)] + ([row(d)] if attn else []) + [p_spec, _const_spec(g.shape)] + (
        [_layer_spec(w_o.shape, attn_layer)] if attn else []) + [
        _layer_spec(w.shape, layer) for w in (w_gate, w_up, w_down, w_ple, w_ple_gate)]
    out_specs = [row(d)]
    out_shape = [jax.ShapeDtypeStruct((n, d), _F32)]
    scratch = []
    prefetch = []
    blocks_per_seq = None
    if pages is not None:
        page_table, cache_kt, cache_vt, floc = pages
        nseq, n_pages = page_table.shape
        npg = PAGES_PER_STEP
        blocks_per_seq, kb = n_pages // npg, npg * PAGE
        assert steps == nseq * blocks_per_seq

        def page_spec(shape, i):
            nd = len(shape) - 1
            return pl.BlockSpec(shape, lambda t, pt_ref: (pt_ref[t * npg + i],) + (0,) * nd)

        key_block = lambda rows: pl.BlockSpec(
            (1, 1, rows, kb), lambda t, pt_ref: (t // blocks_per_seq, t % blocks_per_seq, 0, 0))
        kv_block = (1, N_HEADS, HEAD_DIM, PAGE)
        prefetch = [page_table.reshape(-1)]
        args += [cache_kt] * npg + [cache_vt] * npg + [floc] * npg
        specs += ([page_spec(kv_block, i) for i in range(npg)] * 2
                  + [page_spec((1, N_HEADS, PAGE), i) for i in range(npg)])
        v_block = pl.BlockSpec(
            (1, 1, kb, D_MODEL), lambda t, pt_ref: (t // blocks_per_seq, t % blocks_per_seq, 0, 0))
        out_specs += [key_block(D_MODEL), v_block, key_block(N_HEADS)]
        out_shape += [jax.ShapeDtypeStruct((nseq, blocks_per_seq, D_MODEL, kb), _BF16),
                      jax.ShapeDtypeStruct((nseq, blocks_per_seq, kb, D_MODEL), _BF16),
                      jax.ShapeDtypeStruct((nseq, blocks_per_seq, N_HEADS, kb), _F32)]
        scratch = [pltpu.VMEM((N_HEADS, 1), _F32)]
    grid_spec = pltpu.PrefetchScalarGridSpec(
        num_scalar_prefetch=len(prefetch), grid=(steps,), in_specs=specs, out_specs=out_specs,
        scratch_shapes=scratch)
    outs = pl.pallas_call(
        functools.partial(_ffn_body, attn, blocks_per_seq),
        grid_spec=grid_spec,
        out_shape=out_shape,
        compiler_params=_params(("arbitrary",)),
        name=("ffn_attn" if attn else "ffn") + ("_gather" if pages is not None else ""),
    )(*prefetch, *args)
    return outs[0] if pages is None else outs


def _log_sigmoid(x):
    return jnp.minimum(x, 0.0) - jnp.log1p(jnp.exp(-jnp.abs(x)))


def _store_heads(ref, x):
    for hd in range(N_HEADS):
        ref[0, hd] = x[:, hd * HEAD_SLOT:(hd + 1) * HEAD_SLOT].astype(ref.dtype)


def _kvq_prompt_body(h_ref, gkv_ref, gq_ref, wkt_ref, wvt_ref, wf_ref, bf_ref, wka_ref,
                     wqa_ref, tri_ref, scat_ref, ones_ref,
                     kt_ref, vto_ref, lf_ref, ka_ref, vt_ref, qa_ref, carry_ref):
    @pl.when(pl.program_id(1) == 0)
    def _():
        carry_ref[...] = jnp.zeros_like(carry_ref)

    x = h_ref[0]
    tm = x.shape[0]
    z = _rms(x, gkv_ref[...]).astype(_BF16)
    kt_ref[0] = _dot_nt(wkt_ref[...], z)
    vt = _dot_nt(wvt_ref[...], z)
    vto_ref[0] = vt
    lf = _log_sigmoid(_dot(z, wf_ref[...]) + bf_ref[...])
    lane = lax.broadcasted_iota(jnp.int32, (1, LANES), 1)
    lf = jnp.where(lane < N_HEADS, lf, 0.0)
    lf_ref[0] = lf[:, :N_HEADS]
    hi, mid, lo = _split3(lf)
    tri = tri_ref[...]
    f = _dot(tri, hi) + _dot(tri, mid) + _dot(tri, lo) + carry_ref[0:1, :]
    carry_ref[0:1, :] = f[tm - 1:tm, :]
    nhi, nmid, nlo = _split3(f * -LOG2E)
    parts = (nhi.astype(_F32) + pltpu.roll(nmid.astype(_F32), N_HEADS, 1)
             + pltpu.roll(nlo.astype(_F32), 2 * N_HEADS, 1)).astype(_BF16)
    ka = _dot(z, wka_ref[...]) + _dot(parts, scat_ref[...])
    _store_heads(ka_ref, ka)
    vt_ref[0, :, 0, :HEAD_DIM, :] = vt.astype(_BF16).reshape(N_HEADS, HEAD_DIM, tm)
    extra = lax.broadcasted_iota(jnp.int32, (N_HEADS, VT_ROWS - HEAD_DIM, tm), 1) == 0
    vt_ref[0, :, 0, HEAD_DIM:, :] = extra.astype(_F32).astype(_BF16)
    xq = _rms(x, gq_ref[...]).astype(_BF16)
    _store_heads(qa_ref, _dot(xq, wqa_ref[...]) + ones_ref[...])


def _q_prompt_body(h_ref, gq_ref, wqa_ref, ones_ref, qa_ref):
    xq = _rms(h_ref[0], gq_ref[...]).astype(_BF16)
    _store_heads(qa_ref, _dot(xq, wqa_ref[...]) + ones_ref[...])


def _head_major_spec(tm):
    return pl.BlockSpec((1, N_HEADS, tm, HEAD_SLOT), lambda i, t: (i, 0, t, 0))


def _kvq_prompt(h, gkv, gq, w_kt, w_vt, w_f, b_f, w_ka, w_qa, tri, scat, ones):
    b, s, d = h.shape
    tm = ROW_TILE
    nk = s // tm
    consts = (gkv, gq, w_kt, w_vt, w_f, b_f, w_ka, w_qa, tri, scat, ones)
    tile = lambda c: pl.BlockSpec((1, tm, c), lambda i, t: (i, t, 0))
    tile_t = pl.BlockSpec((1, d, tm), lambda i, t: (i, 0, t))
    hm = jax.ShapeDtypeStruct((b, N_HEADS, s, HEAD_SLOT), _BF16)
    return pl.pallas_call(
        _kvq_prompt_body,
        grid=(b, nk),
        in_specs=[tile(d)] + [_const_spec(c.shape) for c in consts],
        out_specs=[tile_t, tile_t, tile(N_HEADS), _head_major_spec(tm),
                   pl.BlockSpec((1, N_HEADS, 1, VT_ROWS, tm), lambda i, t: (i, 0, t, 0, 0)),
                   _head_major_spec(tm)],
        out_shape=[jax.ShapeDtypeStruct((b, d, s), _F32), jax.ShapeDtypeStruct((b, d, s), _F32),
                   jax.ShapeDtypeStruct((b, s, N_HEADS), _F32), hm,
                   jax.ShapeDtypeStruct((b, N_HEADS, nk, VT_ROWS, tm), _BF16), hm],
        scratch_shapes=[pltpu.VMEM((SUBLANES, LANES), _F32)],
        compiler_params=_params(("arbitrary", "arbitrary")),
        name="kvq_prompt",
    )(h, *consts)


def _q_prompt(h, gq, w_qa, ones):
    b, s, d = h.shape
    tm = ROW_TILE
    consts = (gq, w_qa, ones)
    return pl.pallas_call(
        _q_prompt_body,
        grid=(b, s // tm),
        in_specs=[pl.BlockSpec((1, tm, d), lambda i, t: (i, t, 0))] + [_const_spec(c.shape) for c in consts],
        out_specs=_head_major_spec(tm),
        out_shape=jax.ShapeDtypeStruct((b, N_HEADS, s, HEAD_SLOT), _BF16),
        compiler_params=_params(("arbitrary", "arbitrary")),
        name="q_prompt",
    )(h, *consts)


def _kvq_sample_body(kv, *refs):
    if kv:
        (h_ref, gkv_ref, gq_ref, wk_ref, wv_ref, wf_ref, bf_ref, wq_ref,
         k_ref, v_ref, lf_ref, q_ref) = refs
        x = h_ref[...]
        z = _rms(x, gkv_ref[...]).astype(_BF16)
        k_ref[...] = _dot(z, wk_ref[...])
        v_ref[...] = _dot(z, wv_ref[...])
        lf_ref[...] = _log_sigmoid(_dot(z, wf_ref[...]) + bf_ref[...])[:, :N_HEADS]
    else:
        h_ref, gq_ref, wq_ref, q_ref = refs
        x = h_ref[...]
    q_ref[...] = _dot(_rms(x, gq_ref[...]).astype(_BF16), wq_ref[...])


def _kvq_sample(h, gkv, gq, w_k, w_v, w_f, b_f, w_q):
    n, d = h.shape
    args = (h, gkv, gq, w_k, w_v, w_f, b_f, w_q)
    outs = [(n, d), (n, d), (n, N_HEADS), (n, d)]
    return pl.pallas_call(
        functools.partial(_kvq_sample_body, True),
        grid=(1,),
        in_specs=[_const_spec(a.shape) for a in args],
        out_specs=[_const_spec(o) for o in outs],
        out_shape=[jax.ShapeDtypeStruct(o, _F32) for o in outs],
        compiler_params=_params(("arbitrary",)),
        name="kvq_sample",
    )(*args)


def _q_sample(h, gq, w_q):
    n, d = h.shape
    args = (h, gq, w_q)
    return pl.pallas_call(
        functools.partial(_kvq_sample_body, False),
        grid=(1,),
        in_specs=[_const_spec(a.shape) for a in args],
        out_specs=_const_spec((n, d)),
        out_shape=jax.ShapeDtypeStruct((n, d), _F32),
        compiler_params=_params(("arbitrary",)),
        name="q_sample",
    )(*args)


def _flash_body(q_ref, k_ref, vt_ref, o_ref, s0_ref, s1_ref, m_ref, acc_ref):
    qi = pl.program_id(2)
    tq = q_ref.shape[2]
    tk = vt_ref.shape[4]
    slots = (s0_ref, s1_ref)
    halves = [slice(c, c + FLASH_Q_SPLIT) for c in range(0, tq, FLASH_Q_SPLIT)]

    def scores(hd, j, slot):
        kb = k_ref[0, hd, pl.ds(pl.multiple_of(j * tk, tk), tk), :]
        for cols in halves:
            slots[slot][hd, :, cols] = _dot_nt(kb, q_ref[0, hd, cols, :])

    def update(hd, j, slot, masked):
        for cols in halves:
            keys = slice(0, min(cols.stop, tk)) if masked else slice(0, tk)
            st = slots[slot][hd, keys, cols]
            if masked:
                key_pos = lax.broadcasted_iota(jnp.int32, st.shape, 0)
                qry_pos = lax.broadcasted_iota(jnp.int32, st.shape, 1) + cols.start
                st = jnp.where(key_pos <= qry_pos, st, -jnp.inf)
            m = m_ref[hd, :, cols]
            m_new = jnp.maximum(m, jnp.max(st, axis=0, keepdims=True))
            p = jnp.exp2(st - m_new).astype(_BF16)
            acc_ref[hd, :, cols] = (jnp.exp2(m - m_new) * acc_ref[hd, :, cols]
                                    + _dot(vt_ref[0, hd, j, :, keys], p))
            m_ref[hd, :, cols] = m_new

    m_ref[...] = jnp.full_like(m_ref, -jnp.inf)
    acc_ref[...] = jnp.zeros_like(acc_ref)
    for hd in range(HEAD_GROUP):
        scores(hd, 0, 0)

    def pair(jj, carry):
        j = 2 * jj
        for hd in range(HEAD_GROUP):
            scores(hd, j + 1, 1)
            update(hd, j, 0, False)
        for hd in range(HEAD_GROUP):
            scores(hd, j + 2, 0)
            update(hd, j + 1, 1, False)
        return carry

    lax.fori_loop(0, qi // 2, pair, 0)

    @pl.when(qi % 2 == 0)
    def _():
        for hd in range(HEAD_GROUP):
            update(hd, qi, 0, True)

    @pl.when(qi % 2 == 1)
    def _():
        for hd in range(HEAD_GROUP):
            scores(hd, qi, 1)
            update(hd, qi - 1, 0, False)
        for hd in range(HEAD_GROUP):
            update(hd, qi, 1, True)

    outs = [acc_ref[hd, :HEAD_DIM, :] * (1.0 / acc_ref[hd, HEAD_DIM:HEAD_DIM + 1, :])
            for hd in range(HEAD_GROUP)]
    o_ref[0] = jnp.concatenate(outs, axis=0).T.astype(o_ref.dtype)


def _flash(qa, ka, vt):
    b, _, s, _ = qa.shape
    nk, tk = vt.shape[2], vt.shape[4]
    tq = Q_TILE
    assert tq == tk
    hg = HEAD_GROUP
    return pl.pallas_call(
        _flash_body,
        grid=(b, N_HEADS // hg, s // tq),
        in_specs=[
            pl.BlockSpec((1, hg, tq, HEAD_SLOT), lambda i, g, t: (i, g, t, 0)),
            pl.BlockSpec((1, hg, s, HEAD_SLOT), lambda i, g, t: (i, g, 0, 0)),
            pl.BlockSpec((1, hg, nk, VT_ROWS, tk), lambda i, g, t: (i, g, 0, 0, 0)),
        ],
        out_specs=pl.BlockSpec((1, tq, hg * HEAD_DIM), lambda i, g, t: (i, t, g)),
        out_shape=jax.ShapeDtypeStruct((b, s, D_MODEL), _BF16),
        scratch_shapes=[pltpu.VMEM((hg, tk, tq), _F32), pltpu.VMEM((hg, tk, tq), _F32),
                        pltpu.VMEM((hg, 1, tq), _F32), pltpu.VMEM((hg, VT_ROWS, tq), _F32)],
        compiler_params=_params(("arbitrary", "arbitrary", "arbitrary")),
        name="flash_prompt",
    )(qa, ka, vt)


def _page_cumsum_body(x_ref, triu_ref, o_ref):
    x = x_ref[...].reshape(-1, PAGE)
    hi, mid, lo = _split3(x)
    tri = triu_ref[...]
    o_ref[...] = (_dot(hi, tri) + _dot(mid, tri) + _dot(lo, tri)).reshape(o_ref.shape)


def _page_cumsum(xt, triu, pages):
    n = xt.shape[0]
    spec = pl.BlockSpec((pages, N_HEADS, PAGE), lambda i: (i, 0, 0))
    return pl.pallas_call(
        _page_cumsum_body,
        grid=(n // pages,),
        in_specs=[spec, _const_spec(triu.shape)],
        out_specs=spec,
        out_shape=jax.ShapeDtypeStruct(xt.shape, _F32),
        compiler_params=_params(("arbitrary",)),
        name="page_cumsum",
    )(xt, triu)


def _expand_heads(x):
    n = x.shape[1]
    return jnp.concatenate(
        [jnp.broadcast_to(x[hd:hd + 1, :], (SAMPLE_SEQ, n)) for hd in range(N_HEADS)], axis=0)


class _SampleAttention:
    ROWS = N_HEADS * SAMPLE_SEQ
    CHUNK_HEADS = MXU_DEPTH // HEAD_DIM
    CHUNK_ROWS = CHUNK_HEADS * SAMPLE_SEQ
    CHUNK_COLS = CHUNK_HEADS * HEAD_DIM
    N_CHUNKS = N_HEADS // CHUNK_HEADS
    SCRATCH = [pltpu.VMEM((ROWS, CHUNK_COLS), _BF16), pltpu.VMEM((ROWS, 1), _F32),
               pltpu.VMEM((ROWS, 1), _F32), pltpu.VMEM((ROWS, CHUNK_COLS), _F32)]

    def __init__(self, qbd_ref, m_ref, l_ref, acc_ref):
        self.qbd_ref, self.m_ref, self.l_ref, self.acc_ref = qbd_ref, m_ref, l_ref, acc_ref
        shape = (self.ROWS, self.CHUNK_COLS)
        head_of_row = lax.broadcasted_iota(jnp.int32, shape, 0) // SAMPLE_SEQ % self.CHUNK_HEADS
        head_of_col = lax.broadcasted_iota(jnp.int32, shape, 1) // HEAD_DIM
        self.blockdiag = head_of_row == head_of_col

    def _chunks(self):
        for c in range(self.N_CHUNKS):
            yield (slice(c * self.CHUNK_ROWS, (c + 1) * self.CHUNK_ROWS),
                   slice(c * self.CHUNK_COLS, (c + 1) * self.CHUNK_COLS))

    def start(self, q):
        rows = jnp.concatenate([q[:, cols] for _, cols in self._chunks() for _ in range(self.CHUNK_HEADS)],
                               axis=0)
        self.qbd_ref[...] = jnp.where(self.blockdiag, rows, 0.0).astype(_BF16)
        self.m_ref[...] = jnp.full_like(self.m_ref, -jnp.inf)
        self.l_ref[...] = jnp.zeros_like(self.l_ref)
        self.acc_ref[...] = jnp.zeros_like(self.acc_ref)

    def _attend(self, st, v):
        m = self.m_ref[...]
        m_new = jnp.maximum(m, jnp.max(st, axis=1, keepdims=True))
        alpha = jnp.exp(m - m_new)
        p = jnp.exp(st - m_new)
        self.l_ref[...] = alpha * self.l_ref[...] + jnp.sum(p, axis=1, keepdims=True)
        pb = p.astype(_BF16)
        pv = jnp.concatenate([_dot(pb[rows], v[:, cols]) for rows, cols in self._chunks()], axis=0)
        self.acc_ref[...] = alpha * self.acc_ref[...] + pv
        self.m_ref[...] = m_new

    def past(self, kt, v, f):
        st = jnp.concatenate([_dot(self.qbd_ref[rows, :], kt[cols, :]) for rows, cols in self._chunks()],
                             axis=0)
        self._attend(st - _expand_heads(f), v)

    def finish(self, k_new, v_new, f_new):
        pad = jnp.zeros((PAGE - SAMPLE_SEQ, D_MODEL), _F32)
        kn = jnp.concatenate([k_new, pad], axis=0).astype(_BF16)
        vn = jnp.concatenate([v_new, pad], axis=0).astype(_BF16)
        st = jnp.concatenate([_dot_nt(self.qbd_ref[rows, :], kn[:, cols]) for rows, cols in self._chunks()],
                             axis=0) - _expand_heads(f_new)
        key = lax.broadcasted_iota(jnp.int32, (self.ROWS, PAGE), 1)
        qpos = lax.broadcasted_iota(jnp.int32, (self.ROWS, PAGE), 0) % SAMPLE_SEQ
        self._attend(jnp.where(key <= qpos, st, -jnp.inf), vn)
        full = jnp.where(self.blockdiag, self.acc_ref[...] * (1.0 / self.l_ref[...]), 0.0)
        outs = []
        for rows, _ in self._chunks():
            blk = full[rows]
            out = blk[0:SAMPLE_SEQ]
            for hd in range(1, self.CHUNK_HEADS):
                out = out + blk[hd * SAMPLE_SEQ:(hd + 1) * SAMPLE_SEQ]
            outs.append(out)
        return jnp.concatenate(outs, axis=1)


def _paged_body(steps_per_seq, q_ref, k_ref, v_hbm, f_ref, kn_ref, vn_ref, fn_ref, o_ref,
                vbuf_ref, vsem_ref, *scratch):
    attn = _SampleAttention(*scratch)
    b, j = pl.program_id(0), pl.program_id(1)
    step = b * steps_per_seq + j
    last_step = pl.num_programs(0) * steps_per_seq - 1
    slot = step % 2
    blocks = vbuf_ref.shape[1]

    def v_copy(at_step, into):
        src = v_hbm.at[at_step // steps_per_seq, pl.ds((at_step % steps_per_seq) * blocks, blocks)]
        return pltpu.make_async_copy(src, vbuf_ref.at[into], vsem_ref.at[into])

    @pl.when(step == 0)
    def _():
        v_copy(step, slot).start(priority=1)

    @pl.when(step < last_step)
    def _():
        v_copy(step + 1, 1 - slot).start(priority=1)

    @pl.when(j == 0)
    def _():
        attn.start(q_ref[0])

    v_copy(step, slot).wait()
    for u in range(blocks):
        attn.past(k_ref[0, u], vbuf_ref[slot, u], f_ref[0, u])

    @pl.when(j == steps_per_seq - 1)
    def _():
        f_last = f_ref[0, blocks - 1]
        f_new = fn_ref[0] + f_last[:, f_last.shape[1] - 1:]
        o_ref[0] = attn.finish(kn_ref[0], vn_ref[0], f_new).astype(o_ref.dtype)


def _paged(q, kc, vc, f_past, k_new, v_new, floc_new):
    bd, t, d = q.shape
    assert t == SAMPLE_SEQ
    nblk, kb = kc.shape[1], kc.shape[3]
    per_step = PAGED_KEYS // kb
    steps_per_seq = nblk // per_step
    per_b = lambda shape: pl.BlockSpec(shape, lambda b, j: (b, 0, 0))
    key_blocks = lambda rows: pl.BlockSpec((1, per_step, rows, kb), lambda b, j: (b, j, 0, 0))
    in_specs = [per_b((1, t, d)), key_blocks(d), pl.BlockSpec(memory_space=pltpu.HBM), key_blocks(N_HEADS),
                per_b((1, t, d)), per_b((1, t, d)), per_b((1, N_HEADS, PAGE))]
    return pl.pallas_call(
        functools.partial(_paged_body, steps_per_seq),
        grid=(bd, steps_per_seq),
        in_specs=in_specs,
        out_specs=per_b((1, t, d)),
        out_shape=jax.ShapeDtypeStruct((bd, t, d), _BF16),
        scratch_shapes=[pltpu.VMEM((2, per_step, kb, d), _BF16), pltpu.SemaphoreType.DMA((2,))]
        + _SampleAttention.SCRATCH,
        compiler_params=_params(("arbitrary", "arbitrary")),
        name="paged_sample",
    )(q, kc, vc, f_past, k_new, v_new, floc_new)


def _head_slots(w):
    w = w.reshape(D_MODEL, N_HEADS, HEAD_DIM)
    w = jnp.pad(w, ((0, 0), (0, 0), (0, HEAD_SLOT - HEAD_DIM)))
    return w.reshape(D_MODEL, N_HEADS * HEAD_SLOT)


def _constants():
    tri = np.tril(np.ones((ROW_TILE, ROW_TILE), np.float32))
    triu = np.triu(np.ones((PAGE, PAGE), np.float32))
    scat = np.zeros((LANES, N_HEADS * HEAD_SLOT), np.float32)
    ones = np.zeros((1, N_HEADS * HEAD_SLOT), np.float32)
    for part in range(3):
        for hd in range(N_HEADS):
            scat[part * N_HEADS + hd, hd * HEAD_SLOT + BIAS_LANE + part] = 1.0
            ones[0, hd * HEAD_SLOT + BIAS_LANE + part] = 1.0
    return (jnp.asarray(tri, _BF16), jnp.asarray(triu, _BF16), jnp.asarray(scat, _BF16),
            jnp.asarray(ones, _F32))


def kernel(x_prompt, x_sample, state_conv, cache_k, cache_v, cache_logf, page_table, p_prompt, p_sample,
           norm_g, w_in, w_conv, w_out, g_kv, w_k, w_v, w_f, b_f, w_q, w_o,
           w_gate, w_up, w_down, w_ple, w_ple_gate):
    b, s, d = x_prompt.shape
    bd, t, _ = x_sample.shape
    depth = norm_g.shape[0]
    bf = lambda w: w.astype(_BF16)
    tri, triu, scat, ones = _constants()

    w_k_b, w_v_b, w_kt_b, w_vt_b = bf(w_k), bf(w_v), bf(w_k.T), bf(w_v.T)
    w_ka = bf(_head_slots(w_k))
    w_q_scaled = w_q * SCALE
    w_qa = [bf(_head_slots(w_q_scaled[j] * LOG2E)) for j in range(w_q.shape[0])]
    w_qs = bf(w_q_scaled)
    w_f_pad = bf(jnp.pad(w_f, ((0, 0), (0, LANES - N_HEADS))))
    b_f_pad = jnp.pad(b_f, (0, LANES - N_HEADS)).reshape(1, LANES)
    gkv = g_kv.reshape(1, d)
    w_in_b, w_out_b, w_o_b = bf(w_in), bf(w_out), bf(w_o)
    ffn_w = (bf(w_gate), bf(w_up), bf(w_down), bf(w_ple), bf(w_ple_gate))
    pp = p_prompt.reshape(depth, b * s, PLE_DIM)
    ps = p_sample.reshape(depth, bd * t, PLE_DIM)

    floc = _page_cumsum(jnp.swapaxes(cache_logf, 1, 2), triu, PREPASS_PAGES)
    to_key_minor = lambda c: jnp.transpose(c, (0, 2, 3, 1))
    cache_kt, cache_vt = to_key_minor(cache_k), to_key_minor(cache_v)
    seq_groups = [slice(g * bd // depth, (g + 1) * bd // depth) for g in range(depth)]
    dense = []

    def prompt_ffn(hp, o, i, w_o_stack, jl):
        hp, kc, vc, f_past = _ffn(hp.reshape(b * s, d), o, pp, i, norm_g[i], w_o_stack, jl, *ffn_w,
                                  pages=(page_table[seq_groups[i]], cache_kt, cache_vt, floc))
        dense.append((kc, vc, f_past))
        return hp.reshape(b, s, d)

    hp = x_prompt
    conv_prompt = []
    for i in range(N_CONV_LAYERS):
        hp, st = _conv_prompt(hp, norm_g[i], w_in_b, w_conv[i], w_out_b, i)
        conv_prompt.append(st)
        hp = prompt_ffn(hp, None, i, None, None)
    kt_p, vt_p, lf_p, ka, vt, qa = _kvq_prompt(
        hp, gkv, norm_g[N_CONV_LAYERS, 0:1], w_kt_b, w_vt_b, w_f_pad, b_f_pad, w_ka, w_qa[0],
        tri, scat, ones)
    heads_last = lambda xt: jnp.transpose(xt.reshape(b, N_HEADS, HEAD_DIM, s), (0, 3, 1, 2))
    for i in range(N_CONV_LAYERS, depth):
        jl = i - N_CONV_LAYERS
        if jl > 0:
            qa = _q_prompt(hp, norm_g[i, 0:1], w_qa[jl], ones)
        o = _flash(qa, ka, vt)
        hp = prompt_ffn(hp, o.reshape(b * s, d), i, w_o_b, jl)

    n = bd * t
    hs = x_sample.reshape(n, d)
    zeros_rows = jnp.zeros((bd, t - 2, d), _F32)
    conv_sample = []
    for i in range(N_CONV_LAYERS):
        st = state_conv[i]
        fix1 = jnp.concatenate([st[:, 1:2], st[:, 1:2], zeros_rows], axis=1).reshape(n, d)
        fix2 = jnp.concatenate([st, zeros_rows], axis=1).reshape(n, d)
        hs, cu = _conv_sample(hs, fix1, fix2, norm_g[i], w_in_b, w_conv[i], w_out_b, i)
        conv_sample.append(cu.reshape(bd, t, d)[:, t - 2:])
        hs = _ffn(hs, None, ps, i, norm_g[i], None, None, *ffn_w)
    k_s, v_s, lf_s, q_s = _kvq_sample(hs, gkv, norm_g[N_CONV_LAYERS, 0:1], w_k_b, w_v_b, w_f_pad,
                                      b_f_pad, w_qs[0])
    lf_new = jnp.pad(jnp.swapaxes(lf_s.reshape(bd, t, N_HEADS), 1, 2), ((0, 0), (0, 0), (0, PAGE - t)))
    floc_new = _page_cumsum(lf_new, triu, bd)
    k_new = k_s.reshape(bd, t, d)
    v_new = v_s.reshape(bd, t, d)
    for i in range(N_CONV_LAYERS, depth):
        jl = i - N_CONV_LAYERS
        if jl > 0:
            q_s = _q_sample(hs, norm_g[i, 0:1], w_qs[jl])
        q3 = q_s.reshape(bd, t, d)
        o = jnp.concatenate([_paged(q3[grp], *dense[g], k_new[grp], v_new[grp], floc_new[grp])
                             for g, grp in enumerate(seq_groups)], axis=0)
        hs = _ffn(hs, o.reshape(n, d), ps, i, norm_g[i], w_o_b, jl, *ffn_w)

    return (hp, hs.reshape(bd, t, d), jnp.stack(conv_prompt, axis=0),
            heads_last(kt_p), heads_last(vt_p), lf_p,
            jnp.stack(conv_sample, axis=0),
            k_s.reshape(bd, t, N_HEADS, HEAD_DIM), v_s.reshape(bd, t, N_HEADS, HEAD_DIM),
            lf_s.reshape(bd, t, N_HEADS))
```

```python
import functools

import jax
import jax.numpy as jnp
import numpy as np
from jax import lax
from jax.experimental import pallas as pl
from jax.experimental.pallas import tpu as pltpu

D_MODEL = 1024
N_HEADS = 16
HEAD_DIM = 64
PLE_DIM = 256
EPS = 1e-6
N_CONV_LAYERS = 2
PAGE = 128
LANES = 128
MXU_DEPTH = 256
SUBLANES = 8
BF16_SUBLANES = 16
SAMPLE_SEQ = 8
HEAD_SLOT = 128
BIAS_LANE = HEAD_DIM
SCALE = HEAD_DIM ** -0.5
LOG2E = 1.4426950408889634
VT_ROWS = HEAD_DIM + BF16_SUBLANES

ROW_TILE = 512
FFN_TILE = 256
Q_TILE = 512
FLASH_Q_SPLIT = 256
HEAD_GROUP = 4
PAGES_PER_STEP = 8
PAGED_KEYS = 4096
PREPASS_PAGES = 256
VMEM_LIMIT = 56 * 1024 * 1024

_F32 = jnp.float32
_BF16 = jnp.bfloat16


def _rms(x, g):
    return x * lax.rsqrt(jnp.mean(x * x, axis=-1, keepdims=True) + EPS) * g


def _dot(a, b):
    return jnp.dot(a, b, preferred_element_type=_F32)


def _dot_nt(a, b):
    return lax.dot_general(a, b, (((1,), (1,)), ((), ())), preferred_element_type=_F32)


def _split3(x):
    hi = x.astype(_BF16)
    r = x - hi.astype(_F32)
    mid = r.astype(_BF16)
    lo = (r - mid.astype(_F32)).astype(_BF16)
    return hi, mid, lo


def _const_spec(shape):
    nd = len(shape)
    return pl.BlockSpec(shape, lambda *_: (0,) * nd, pipeline_mode=pl.Buffered(1))


def _layer_spec(shape, layer):
    nd = len(shape) - 1
    return pl.BlockSpec((None,) + tuple(shape[1:]), lambda *_: (layer,) + (0,) * nd,
                        pipeline_mode=pl.Buffered(1))


def _params(sem):
    return pltpu.CompilerParams(dimension_semantics=sem, vmem_limit_bytes=VMEM_LIMIT)


def _conv_body(sample, *refs):
    if sample:
        (h_ref, fix1_ref, fix2_ref, g_ref, win_ref, wconv_ref, wout_ref, out_ref, cu_ref) = refs
    else:
        (h_ref, g_ref, win_ref, wconv_ref, wout_ref, out_ref, state_ref, carry_ref) = refs

        @pl.when(pl.program_id(1) == 0)
        def _():
            carry_ref[...] = jnp.zeros_like(carry_ref)

    x = h_ref[...].reshape(h_ref.shape[-2:])
    tm = x.shape[0]
    xn = _rms(x, g_ref[0:1, :]).astype(_BF16)
    proj = _dot(xn, win_ref[...])
    bg = proj[:, :D_MODEL]
    cu = proj[:, D_MODEL:2 * D_MODEL] * proj[:, 2 * D_MODEL:]
    s1 = pltpu.roll(cu, 1, 0)
    s2 = pltpu.roll(cu, 2, 0)
    row = lax.broadcasted_iota(jnp.int32, (tm, 1), 0)
    if sample:
        pos = row % SAMPLE_SEQ
        s1 = jnp.where(pos == 0, fix1_ref[...], s1)
        s2 = jnp.where(pos < 2, fix2_ref[...], s2)
    else:
        prev2 = carry_ref[0:1, :]
        prev1 = carry_ref[1:2, :]
        s1 = jnp.where(row == 0, prev1, s1)
        s2 = jnp.where(row == 0, prev2, jnp.where(row == 1, prev1, s2))
    conv = wconv_ref[0:1, :] * s2 + wconv_ref[1:2, :] * s1 + wconv_ref[2:3, :] * cu
    y = _dot((bg * conv).astype(_BF16), wout_ref[...])
    out = x + _rms(y, g_ref[1:2, :])
    out_ref[...] = out.reshape(out_ref.shape)
    if sample:
        cu_ref[...] = cu
    else:
        tail = cu[tm - 2:tm, :]
        carry_ref[0:2, :] = tail
        state_ref[0] = tail


def _conv_prompt(h, g, w_in, w_conv, w_out, layer):
    b, s, d = h.shape
    tm = ROW_TILE
    return pl.pallas_call(
        functools.partial(_conv_body, False),
        grid=(b, s // tm),
        in_specs=[
            pl.BlockSpec((1, tm, d), lambda i, t: (i, t, 0)),
            _const_spec(g.shape), _layer_spec(w_in.shape, layer), _const_spec(w_conv.shape),
            _layer_spec(w_out.shape, layer),
        ],
        out_specs=[
            pl.BlockSpec((1, tm, d), lambda i, t: (i, t, 0)),
            pl.BlockSpec((1, 2, d), lambda i, t: (i, 0, 0)),
        ],
        out_shape=[jax.ShapeDtypeStruct(h.shape, _F32), jax.ShapeDtypeStruct((b, 2, d), _F32)],
        scratch_shapes=[pltpu.VMEM((SUBLANES, d), _F32)],
        compiler_params=_params(("arbitrary", "arbitrary")),
        name="conv_prompt",
    )(h, g, w_in, w_conv, w_out)


def _conv_sample(h, fix1, fix2, g, w_in, w_conv, w_out, layer):
    n, d = h.shape
    args = (h, fix1, fix2, g, w_in, w_conv, w_out)
    specs = [_const_spec(a.shape) for a in args]
    specs[4] = _layer_spec(w_in.shape, layer)
    specs[6] = _layer_spec(w_out.shape, layer)
    return pl.pallas_call(
        functools.partial(_conv_body, True),
        grid=(1,),
        in_specs=specs,
        out_specs=[_const_spec((n, d)), _const_spec((n, d))],
        out_shape=[jax.ShapeDtypeStruct((n, d), _F32), jax.ShapeDtypeStruct((n, d), _F32)],
        compiler_params=_params(("arbitrary",)),
        name="conv_sample",
    )(*args)


def _gather_pages(k_refs, v_refs, f_refs, kc_ref, vc_ref, fo_ref, carry_ref, first_block):
    c = jnp.where(first_block, 0.0, carry_ref[...])
    for i in range(PAGES_PER_STEP):
        keys = slice(i * PAGE, (i + 1) * PAGE)
        kc_ref[0, 0, :, keys] = k_refs[i][0].reshape(D_MODEL, PAGE).astype(_BF16)
        vc_ref[0, 0, keys, :] = v_refs[i][0].reshape(D_MODEL, PAGE).T.astype(_BF16)
        f = f_refs[i][0] + c
        fo_ref[0, 0, :, keys] = f
        c = f[:, PAGE - 1:PAGE]
    carry_ref[...] = c


def _ffn_body(attn, blocks_per_seq, *refs):
    npg = PAGES_PER_STEP
    gather = blocks_per_seq is not None
    if gather:
        nb = 10 if attn else 8
        refs = refs[1:]
        k_refs, v_refs, f_refs = (refs[nb + i * npg:nb + (i + 1) * npg] for i in range(3))
        out_ref, kc_ref, vc_ref, fo_ref, carry_ref = refs[nb + 3 * npg:]
        refs = refs[:nb] + (out_ref,)
    if attn:
        (h_ref, o_ref, p_ref, g_ref, wo_ref, wg_ref, wu_ref, wd_ref, wple_ref, wpg_ref, out_ref) = refs
    else:
        (h_ref, p_ref, g_ref, wg_ref, wu_ref, wd_ref, wple_ref, wpg_ref, out_ref) = refs
    x = h_ref[...]
    if attn:
        x = x + _rms(_dot(o_ref[...], wo_ref[...]), g_ref[1:2, :])
    xn = _rms(x, g_ref[2:3, :]).astype(_BF16)
    gate = _dot(xn, wg_ref[...])
    up = _dot(xn, wu_ref[...])
    act = (gate * jax.nn.sigmoid(gate) * up).astype(_BF16)
    x = x + _rms(_dot(act, wd_ref[...]), g_ref[3:4, :])
    pg = jax.nn.sigmoid(_dot(x.astype(_BF16), wpg_ref[...]))
    pe = _dot(p_ref[...].astype(_BF16), wple_ref[...])
    out_ref[...] = x + pg * pe
    if gather:
        _gather_pages(k_refs, v_refs, f_refs, kc_ref, vc_ref, fo_ref, carry_ref,
                      pl.program_id(0) % blocks_per_seq == 0)


def _ffn(h, o, p, layer, g, w_o, attn_layer, w_gate, w_up, w_down, w_ple, w_ple_gate, pages=None):
    n, d = h.shape
    tm = min(FFN_TILE, n)
    steps = n // tm
    row = lambda c: pl.BlockSpec((tm, c), lambda t, *_: (t, 0))
    p_spec = pl.BlockSpec((None, tm, p.shape[2]), lambda t, *_: (layer, t, 0))
    attn = o is not None
    args = [h] + ([o] if attn else []) + [p, g] + ([w_o] if attn else []) + [
        w_gate, w_up, w_down, w_ple, w_ple_gate]
    specs = [row(d)] + ([row(d)] if attn else []) + [p_spec, _const_spec(g.shape)] + (
        [_layer_spec(w_o.shape, attn_layer)] if attn else []) + [
        _layer_spec(w.shape, layer) for w in (w_gate, w_up, w_down, w_ple, w_ple_gate)]
    out_specs = [row(d)]
    out_shape = [jax.ShapeDtypeStruct((n, d), _F32)]
    scratch = []
    prefetch = []
    blocks_per_seq = None
    if pages is not None:
        page_table, cache_kt, cache_vt, floc = pages
        nseq, n_pages = page_table.shape
        npg = PAGES_PER_STEP
        blocks_per_seq, kb = n_pages // npg, npg * PAGE
        assert steps == nseq * blocks_per_seq

        def page_spec(shape, i):
            nd = len(shape) - 1
            return pl.BlockSpec(shape, lambda t, pt_ref: (pt_ref[t * npg + i],) + (0,) * nd)

        key_block = lambda rows: pl.BlockSpec(
            (1, 1, rows, kb), lambda t, pt_ref: (t // blocks_per_seq, t % blocks_per_seq, 0, 0))
        kv_block = (1, N_HEADS, HEAD_DIM, PAGE)
        prefetch = [page_table.reshape(-1)]
        args += [cache_kt] * npg + [cache_vt] * npg + [floc] * npg
        specs += ([page_spec(kv_block, i) for i in range(npg)] * 2
                  + [page_spec((1, N_HEADS, PAGE), i) for i in range(npg)])
        v_block = pl.BlockSpec(
            (1, 1, kb, D_MODEL), lambda t, pt_ref: (t // blocks_per_seq, t % blocks_per_seq, 0, 0))
        out_specs += [key_block(D_MODEL), v_block, key_block(N_HEADS)]
        out_shape += [jax.ShapeDtypeStruct((nseq, blocks_per_seq, D_MODEL, kb), _BF16),
                      jax.ShapeDtypeStruct((nseq, blocks_per_seq, kb, D_MODEL), _BF16),
                      jax.ShapeDtypeStruct((nseq, blocks_per_seq, N_HEADS, kb), _F32)]
        scratch = [pltpu.VMEM((N_HEADS, 1), _F32)]
    grid_spec = pltpu.PrefetchScalarGridSpec(
        num_scalar_prefetch=len(prefetch), grid=(steps,), in_specs=specs, out_specs=out_specs,
        scratch_shapes=scratch)
    outs = pl.pallas_call(
        functools.partial(_ffn_body, attn, blocks_per_seq),
        grid_spec=grid_spec,
        out_shape=out_shape,
        compiler_params=_params(("arbitrary",)),
        name=("ffn_attn" if attn else "ffn") + ("_gather" if pages is not None else ""),
    )(*prefetch, *args)
    return outs[0] if pages is None else outs


def _log_sigmoid(x):
    return jnp.minimum(x, 0.0) - jnp.log1p(jnp.exp(-jnp.abs(x)))


def _store_heads(ref, x):
    for hd in range(N_HEADS):
        ref[0, hd] = x[:, hd * HEAD_SLOT:(hd + 1) * HEAD_SLOT].astype(ref.dtype)


def _kvq_prompt_body(h_ref, gkv_ref, gq_ref, wkt_ref, wvt_ref, wf_ref, bf_ref, wka_ref,
                     wqa_ref, tri_ref, scat_ref, ones_ref,
                     kt_ref, vto_ref, lf_ref, ka_ref, vt_ref, qa_ref, carry_ref):
    @pl.when(pl.program_id(1) == 0)
    def _():
        carry_ref[...] = jnp.zeros_like(carry_ref)

    x = h_ref[0]
    tm = x.shape[0]
    z = _rms(x, gkv_ref[...]).astype(_BF16)
    kt_ref[0] = _dot_nt(wkt_ref[...], z)
    vt = _dot_nt(wvt_ref[...], z)
    vto_ref[0] = vt
    lf = _log_sigmoid(_dot(z, wf_ref[...]) + bf_ref[...])
    lane = lax.broadcasted_iota(jnp.int32, (1, LANES), 1)
    lf = jnp.where(lane < N_HEADS, lf, 0.0)
    lf_ref[0] = lf[:, :N_HEADS]
    hi, mid, lo = _split3(lf)
    tri = tri_ref[...]
    f = _dot(tri, hi) + _dot(tri, mid) + _dot(tri, lo) + carry_ref[0:1, :]
    carry_ref[0:1, :] = f[tm - 1:tm, :]
    nhi, nmid, nlo = _split3(f * -LOG2E)
    parts = (nhi.astype(_F32) + pltpu.roll(nmid.astype(_F32), N_HEADS, 1)
             + pltpu.roll(nlo.astype(_F32), 2 * N_HEADS, 1)).astype(_BF16)
    ka = _dot(z, wka_ref[...]) + _dot(parts, scat_ref[...])
    _store_heads(ka_ref, ka)
    vt_ref[0, :, 0, :HEAD_DIM, :] = vt.astype(_BF16).reshape(N_HEADS, HEAD_DIM, tm)
    extra = lax.broadcasted_iota(jnp.int32, (N_HEADS, VT_ROWS - HEAD_DIM, tm), 1) == 0
    vt_ref[0, :, 0, HEAD_DIM:, :] = extra.astype(_F32).astype(_BF16)
    xq = _rms(x, gq_ref[...]).astype(_BF16)
    _store_heads(qa_ref, _dot(xq, wqa_ref[...]) + ones_ref[...])


def _q_prompt_body(h_ref, gq_ref, wqa_ref, ones_ref, qa_ref):
    xq = _rms(h_ref[0], gq_ref[...]).astype(_BF16)
    _store_heads(qa_ref, _dot(xq, wqa_ref[...]) + ones_ref[...])


def _head_major_spec(tm):
    return pl.BlockSpec((1, N_HEADS, tm, HEAD_SLOT), lambda i, t: (i, 0, t, 0))


def _kvq_prompt(h, gkv, gq, w_kt, w_vt, w_f, b_f, w_ka, w_qa, tri, scat, ones):
    b, s, d = h.shape
    tm = ROW_TILE
    nk = s // tm
    consts = (gkv, gq, w_kt, w_vt, w_f, b_f, w_ka, w_qa, tri, scat, ones)
    tile = lambda c: pl.BlockSpec((1, tm, c), lambda i, t: (i, t, 0))
    tile_t = pl.BlockSpec((1, d, tm), lambda i, t: (i, 0, t))
    hm = jax.ShapeDtypeStruct((b, N_HEADS, s, HEAD_SLOT), _BF16)
    return pl.pallas_call(
        _kvq_prompt_body,
        grid=(b, nk),
        in_specs=[tile(d)] + [_const_spec(c.shape) for c in consts],
        out_specs=[tile_t, tile_t, tile(N_HEADS), _head_major_spec(tm),
                   pl.BlockSpec((1, N_HEADS, 1, VT_ROWS, tm), lambda i, t: (i, 0, t, 0, 0)),
                   _head_major_spec(tm)],
        out_shape=[jax.ShapeDtypeStruct((b, d, s), _F32), jax.ShapeDtypeStruct((b, d, s), _F32),
                   jax.ShapeDtypeStruct((b, s, N_HEADS), _F32), hm,
                   jax.ShapeDtypeStruct((b, N_HEADS, nk, VT_ROWS, tm), _BF16), hm],
        scratch_shapes=[pltpu.VMEM((SUBLANES, LANES), _F32)],
        compiler_params=_params(("arbitrary", "arbitrary")),
        name="kvq_prompt",
    )(h, *consts)


def _q_prompt(h, gq, w_qa, ones):
    b, s, d = h.shape
    tm = ROW_TILE
    consts = (gq, w_qa, ones)
    return pl.pallas_call(
        _q_prompt_body,
        grid=(b, s // tm),
        in_specs=[pl.BlockSpec((1, tm, d), lambda i, t: (i, t, 0))] + [_const_spec(c.shape) for c in consts],
        out_specs=_head_major_spec(tm),
        out_shape=jax.ShapeDtypeStruct((b, N_HEADS, s, HEAD_SLOT), _BF16),
        compiler_params=_params(("arbitrary", "arbitrary")),
        name="q_prompt",
    )(h, *consts)


def _kvq_sample_body(kv, *refs):
    if kv:
        (h_ref, gkv_ref, gq_ref, wk_ref, wv_ref, wf_ref, bf_ref, wq_ref,
         k_ref, v_ref, lf_ref, q_ref) = refs
        x = h_ref[...]
        z = _rms(x, gkv_ref[...]).astype(_BF16)
        k_ref[...] = _dot(z, wk_ref[...])
        v_ref[...] = _dot(z, wv_ref[...])
        lf_ref[...] = _log_sigmoid(_dot(z, wf_ref[...]) + bf_ref[...])[:, :N_HEADS]
    else:
        h_ref, gq_ref, wq_ref, q_ref = refs
        x = h_ref[...]
    q_ref[...] = _dot(_rms(x, gq_ref[...]).astype(_BF16), wq_ref[...])


def _kvq_sample(h, gkv, gq, w_k, w_v, w_f, b_f, w_q):
    n, d = h.shape
    args = (h, gkv, gq, w_k, w_v, w_f, b_f, w_q)
    outs = [(n, d), (n, d), (n, N_HEADS), (n, d)]
    return pl.pallas_call(
        functools.partial(_kvq_sample_body, True),
        grid=(1,),
        in_specs=[_const_spec(a.shape) for a in args],
        out_specs=[_const_spec(o) for o in outs],
        out_shape=[jax.ShapeDtypeStruct(o, _F32) for o in outs],
        compiler_params=_params(("arbitrary",)),
        name="kvq_sample",
    )(*args)


def _q_sample(h, gq, w_q):
    n, d = h.shape
    args = (h, gq, w_q)
    return pl.pallas_call(
        functools.partial(_kvq_sample_body, False),
        grid=(1,),
        in_specs=[_const_spec(a.shape) for a in args],
        out_specs=_const_spec((n, d)),
        out_shape=jax.ShapeDtypeStruct((n, d), _F32),
        compiler_params=_params(("arbitrary",)),
        name="q_sample",
    )(*args)


def _flash_body(q_ref, k_ref, vt_ref, o_ref, s0_ref, s1_ref, m_ref, acc_ref):
    qi = pl.program_id(2)
    tq = q_ref.shape[2]
    tk = vt_ref.shape[4]
    slots = (s0_ref, s1_ref)
    halves = [slice(c, c + FLASH_Q_SPLIT) for c in range(0, tq, FLASH_Q_SPLIT)]

    def scores(hd, j, slot):
        kb = k_ref[0, hd, pl.ds(pl.multiple_of(j * tk, tk), tk), :]
        for cols in halves:
            slots[slot][hd, :, cols] = _dot_nt(kb, q_ref[0, hd, cols, :])

    def update(hd, j, slot, masked):
        for cols in halves:
            keys = slice(0, min(cols.stop, tk)) if masked else slice(0, tk)
            st = slots[slot][hd, keys, cols]
            if masked:
                key_pos = lax.broadcasted_iota(jnp.int32, st.shape, 0)
                qry_pos = lax.broadcasted_iota(jnp.int32, st.shape, 1) + cols.start
                st = jnp.where(key_pos <= qry_pos, st, -jnp.inf)
            m = m_ref[hd, :, cols]
            m_new = jnp.maximum(m, jnp.max(st, axis=0, keepdims=True))
            p = jnp.exp2(st - m_new).astype(_BF16)
            acc_ref[hd, :, cols] = (jnp.exp2(m - m_new) * acc_ref[hd, :, cols]
                                    + _dot(vt_ref[0, hd, j, :, keys], p))
            m_ref[hd, :, cols] = m_new

    m_ref[...] = jnp.full_like(m_ref, -jnp.inf)
    acc_ref[...] = jnp.zeros_like(acc_ref)
    for hd in range(HEAD_GROUP):
        scores(hd, 0, 0)

    def pair(jj, carry):
        j = 2 * jj
        for hd in range(HEAD_GROUP):
            scores(hd, j + 1, 1)
            update(hd, j, 0, False)
        for hd in range(HEAD_GROUP):
            scores(hd, j + 2, 0)
            update(hd, j + 1, 1, False)
        return carry

    lax.fori_loop(0, qi // 2, pair, 0)

    @pl.when(qi % 2 == 0)
    def _():
        for hd in range(HEAD_GROUP):
            update(hd, qi, 0, True)

    @pl.when(qi % 2 == 1)
    def _():
        for hd in range(HEAD_GROUP):
            scores(hd, qi, 1)
            update(hd, qi - 1, 0, False)
        for hd in range(HEAD_GROUP):
            update(hd, qi, 1, True)

    outs = [acc_ref[hd, :HEAD_DIM, :] * (1.0 / acc_ref[hd, HEAD_DIM:HEAD_DIM + 1, :])
            for hd in range(HEAD_GROUP)]
    o_ref[0] = jnp.concatenate(outs, axis=0).T.astype(o_ref.dtype)


def _flash(qa, ka, vt):
    b, _, s, _ = qa.shape
    nk, tk = vt.shape[2], vt.shape[4]
    tq = Q_TILE
    assert tq == tk
    hg = HEAD_GROUP
    return pl.pallas_call(
        _flash_body,
        grid=(b, N_HEADS // hg, s // tq),
        in_specs=[
            pl.BlockSpec((1, hg, tq, HEAD_SLOT), lambda i, g, t: (i, g, t, 0)),
            pl.BlockSpec((1, hg, s, HEAD_SLOT), lambda i, g, t: (i, g, 0, 0)),
            pl.BlockSpec((1, hg, nk, VT_ROWS, tk), lambda i, g, t: (i, g, 0, 0, 0)),
        ],
        out_specs=pl.BlockSpec((1, tq, hg * HEAD_DIM), lambda i, g, t: (i, t, g)),
        out_shape=jax.ShapeDtypeStruct((b, s, D_MODEL), _BF16),
        scratch_shapes=[pltpu.VMEM((hg, tk, tq), _F32), pltpu.VMEM((hg, tk, tq), _F32),
                        pltpu.VMEM((hg, 1, tq), _F32), pltpu.VMEM((hg, VT_ROWS, tq), _F32)],
        compiler_params=_params(("arbitrary", "arbitrary", "arbitrary")),
        name="flash_prompt",
    )(qa, ka, vt)


def _page_cumsum_body(x_ref, triu_ref, o_ref):
    x = x_ref[...].reshape(-1, PAGE)
    hi, mid, lo = _split3(x)
    tri = triu_ref[...]
    o_ref[...] = (_dot(hi, tri) + _dot(mid, tri) + _dot(lo, tri)).reshape(o_ref.shape)


def _page_cumsum(xt, triu, pages):
    n = xt.shape[0]
    spec = pl.BlockSpec((pages, N_HEADS, PAGE), lambda i: (i, 0, 0))
    return pl.pallas_call(
        _page_cumsum_body,
        grid=(n // pages,),
        in_specs=[spec, _const_spec(triu.shape)],
        out_specs=spec,
        out_shape=jax.ShapeDtypeStruct(xt.shape, _F32),
        compiler_params=_params(("arbitrary",)),
        name="page_cumsum",
    )(xt, triu)


def _expand_heads(x):
    n = x.shape[1]
    return jnp.concatenate(
        [jnp.broadcast_to(x[hd:hd + 1, :], (SAMPLE_SEQ, n)) for hd in range(N_HEADS)], axis=0)


class _SampleAttention:
    ROWS = N_HEADS * SAMPLE_SEQ
    CHUNK_HEADS = MXU_DEPTH // HEAD_DIM
    CHUNK_ROWS = CHUNK_HEADS * SAMPLE_SEQ
    CHUNK_COLS = CHUNK_HEADS * HEAD_DIM
    N_CHUNKS = N_HEADS // CHUNK_HEADS
    SCRATCH = [pltpu.VMEM((ROWS, CHUNK_COLS), _BF16), pltpu.VMEM((ROWS, 1), _F32),
               pltpu.VMEM((ROWS, 1), _F32), pltpu.VMEM((ROWS, CHUNK_COLS), _F32)]

    def __init__(self, qbd_ref, m_ref, l_ref, acc_ref):
        self.qbd_ref, self.m_ref, self.l_ref, self.acc_ref = qbd_ref, m_ref, l_ref, acc_ref
        shape = (self.ROWS, self.CHUNK_COLS)
        head_of_row = lax.broadcasted_iota(jnp.int32, shape, 0) // SAMPLE_SEQ % self.CHUNK_HEADS
        head_of_col = lax.broadcasted_iota(jnp.int32, shape, 1) // HEAD_DIM
        self.blockdiag = head_of_row == head_of_col

    def _chunks(self):
        for c in range(self.N_CHUNKS):
            yield (slice(c * self.CHUNK_ROWS, (c + 1) * self.CHUNK_ROWS),
                   slice(c * self.CHUNK_COLS, (c + 1) * self.CHUNK_COLS))

    def start(self, q):
        rows = jnp.concatenate([q[:, cols] for _, cols in self._chunks() for _ in range(self.CHUNK_HEADS)],
                               axis=0)
        self.qbd_ref[...] = jnp.where(self.blockdiag, rows, 0.0).astype(_BF16)
        self.m_ref[...] = jnp.full_like(self.m_ref, -jnp.inf)
        self.l_ref[...] = jnp.zeros_like(self.l_ref)
        self.acc_ref[...] = jnp.zeros_like(self.acc_ref)

    def _attend(self, st, v):
        m = self.m_ref[...]
        m_new = jnp.maximum(m, jnp.max(st, axis=1, keepdims=True))
        alpha = jnp.exp(m - m_new)
        p = jnp.exp(st - m_new)
        self.l_ref[...] = alpha * self.l_ref[...] + jnp.sum(p, axis=1, keepdims=True)
        pb = p.astype(_BF16)
        pv = jnp.concatenate([_dot(pb[rows], v[:, cols]) for rows, cols in self._chunks()], axis=0)
        self.acc_ref[...] = alpha * self.acc_ref[...] + pv
        self.m_ref[...] = m_new

    def past(self, kt, v, f):
        st = jnp.concatenate([_dot(self.qbd_ref[rows, :], kt[cols, :]) for rows, cols in self._chunks()],
                             axis=0)
        self._attend(st - _expand_heads(f), v)

    def finish(self, k_new, v_new, f_new):
        pad = jnp.zeros((PAGE - SAMPLE_SEQ, D_MODEL), _F32)
        kn = jnp.concatenate([k_new, pad], axis=0).astype(_BF16)
        vn = jnp.concatenate([v_new, pad], axis=0).astype(_BF16)
        st = jnp.concatenate([_dot_nt(self.qbd_ref[rows, :], kn[:, cols]) for rows, cols in self._chunks()],
                             axis=0) - _expand_heads(f_new)
        key = lax.broadcasted_iota(jnp.int32, (self.ROWS, PAGE), 1)
        qpos = lax.broadcasted_iota(jnp.int32, (self.ROWS, PAGE), 0) % SAMPLE_SEQ
        self._attend(jnp.where(key <= qpos, st, -jnp.inf), vn)
        full = jnp.where(self.blockdiag, self.acc_ref[...] * (1.0 / self.l_ref[...]), 0.0)
        outs = []
        for rows, _ in self._chunks():
            blk = full[rows]
            out = blk[0:SAMPLE_SEQ]
            for hd in range(1, self.CHUNK_HEADS):
                out = out + blk[hd * SAMPLE_SEQ:(hd + 1) * SAMPLE_SEQ]
            outs.append(out)
        return jnp.concatenate(outs, axis=1)


def _paged_body(steps_per_seq, q_ref, k_ref, v_hbm, f_ref, kn_ref, vn_ref, fn_ref, o_ref,
                vbuf_ref, vsem_ref, *scratch):
    attn = _SampleAttention(*scratch)
    b, j = pl.program_id(0), pl.program_id(1)
    step = b * steps_per_seq + j
    last_step = pl.num_programs(0) * steps_per_seq - 1
    slot = step % 2
    blocks = vbuf_ref.shape[1]

    def v_copy(at_step, into):
        src = v_hbm.at[at_step // steps_per_seq, pl.ds((at_step % steps_per_seq) * blocks, blocks)]
        return pltpu.make_async_copy(src, vbuf_ref.at[into], vsem_ref.at[into])

    @pl.when(step == 0)
    def _():
        v_copy(step, slot).start(priority=1)

    @pl.when(step < last_step)
    def _():
        v_copy(step + 1, 1 - slot).start(priority=1)

    @pl.when(j == 0)
    def _():
        attn.start(q_ref[0])

    v_copy(step, slot).wait()
    for u in range(blocks):
        attn.past(k_ref[0, u], vbuf_ref[slot, u], f_ref[0, u])

    @pl.when(j == steps_per_seq - 1)
    def _():
        f_last = f_ref[0, blocks - 1]
        f_new = fn_ref[0] + f_last[:, f_last.shape[1] - 1:]
        o_ref[0] = attn.finish(kn_ref[0], vn_ref[0], f_new).astype(o_ref.dtype)


def _paged(q, kc, vc, f_past, k_new, v_new, floc_new):
    bd, t, d = q.shape
    assert t == SAMPLE_SEQ
    nblk, kb = kc.shape[1], kc.shape[3]
    per_step = PAGED_KEYS // kb
    steps_per_seq = nblk // per_step
    per_b = lambda shape: pl.BlockSpec(shape, lambda b, j: (b, 0, 0))
    key_blocks = lambda rows: pl.BlockSpec((1, per_step, rows, kb), lambda b, j: (b, j, 0, 0))
    in_specs = [per_b((1, t, d)), key_blocks(d), pl.BlockSpec(memory_space=pltpu.HBM), key_blocks(N_HEADS),
                per_b((1, t, d)), per_b((1, t, d)), per_b((1, N_HEADS, PAGE))]
    return pl.pallas_call(
        functools.partial(_paged_body, steps_per_seq),
        grid=(bd, steps_per_seq),
        in_specs=in_specs,
        out_specs=per_b((1, t, d)),
        out_shape=jax.ShapeDtypeStruct((bd, t, d), _BF16),
        scratch_shapes=[pltpu.VMEM((2, per_step, kb, d), _BF16), pltpu.SemaphoreType.DMA((2,))]
        + _SampleAttention.SCRATCH,
        compiler_params=_params(("arbitrary", "arbitrary")),
        name="paged_sample",
    )(q, kc, vc, f_past, k_new, v_new, floc_new)


def _head_slots(w):
    w = w.reshape(D_MODEL, N_HEADS, HEAD_DIM)
    w = jnp.pad(w, ((0, 0), (0, 0), (0, HEAD_SLOT - HEAD_DIM)))
    return w.reshape(D_MODEL, N_HEADS * HEAD_SLOT)


def _constants():
    tri = np.tril(np.ones((ROW_TILE, ROW_TILE), np.float32))
    triu = np.triu(np.ones((PAGE, PAGE), np.float32))
    scat = np.zeros((LANES, N_HEADS * HEAD_SLOT), np.float32)
    ones = np.zeros((1, N_HEADS * HEAD_SLOT), np.float32)
    for part in range(3):
        for hd in range(N_HEADS):
            scat[part * N_HEADS + hd, hd * HEAD_SLOT + BIAS_LANE + part] = 1.0
            ones[0, hd * HEAD_SLOT + BIAS_LANE + part] = 1.0
    return (jnp.asarray(tri, _BF16), jnp.asarray(triu, _BF16), jnp.asarray(scat, _BF16),
            jnp.asarray(ones, _F32))


def kernel(x_prompt, x_sample, state_conv, cache_k, cache_v, cache_logf, page_table, p_prompt, p_sample,
           norm_g, w_in, w_conv, w_out, g_kv, w_k, w_v, w_f, b_f, w_q, w_o,
           w_gate, w_up, w_down, w_ple, w_ple_gate):
    b, s, d = x_prompt.shape
    bd, t, _ = x_sample.shape
    depth = norm_g.shape[0]
    bf = lambda w: w.astype(_BF16)
    tri, triu, scat, ones = _constants()

    w_k_b, w_v_b, w_kt_b, w_vt_b = bf(w_k), bf(w_v), bf(w_k.T), bf(w_v.T)
    w_ka = bf(_head_slots(w_k))
    w_q_scaled = w_q * SCALE
    w_qa = [bf(_head_slots(w_q_scaled[j] * LOG2E)) for j in range(w_q.shape[0])]
    w_qs = bf(w_q_scaled)
    w_f_pad = bf(jnp.pad(w_f, ((0, 0), (0, LANES - N_HEADS))))
    b_f_pad = jnp.pad(b_f, (0, LANES - N_HEADS)).reshape(1, LANES)
    gkv = g_kv.reshape(1, d)
    w_in_b, w_out_b, w_o_b = bf(w_in), bf(w_out), bf(w_o)
    ffn_w = (bf(w_gate), bf(w_up), bf(w_down), bf(w_ple), bf(w_ple_gate))
    pp = p_prompt.reshape(depth, b * s, PLE_DIM)
    ps = p_sample.reshape(depth, bd * t, PLE_DIM)

    floc = _page_cumsum(jnp.swapaxes(cache_logf, 1, 2), triu, PREPASS_PAGES)
    to_key_minor = lambda c: jnp.transpose(c, (0, 2, 3, 1))
    cache_kt, cache_vt = to_key_minor(cache_k), to_key_minor(cache_v)
    seq_groups = [slice(g * bd // depth, (g + 1) * bd // depth) for g in range(depth)]
    dense = []

    def prompt_ffn(hp, o, i, w_o_stack, jl):
        hp, kc, vc, f_past = _ffn(hp.reshape(b * s, d), o, pp, i, norm_g[i], w_o_stack, jl, *ffn_w,
                                  pages=(page_table[seq_groups[i]], cache_kt, cache_vt, floc))
        dense.append((kc, vc, f_past))
        return hp.reshape(b, s, d)

    hp = x_prompt
    conv_prompt = []
    for i in range(N_CONV_LAYERS):
        hp, st = _conv_prompt(hp, norm_g[i], w_in_b, w_conv[i], w_out_b, i)
        conv_prompt.append(st)
        hp = prompt_ffn(hp, None, i, None, None)
    kt_p, vt_p, lf_p, ka, vt, qa = _kvq_prompt(
        hp, gkv, norm_g[N_CONV_LAYERS, 0:1], w_kt_b, w_vt_b, w_f_pad, b_f_pad, w_ka, w_qa[0],
        tri, scat, ones)
    heads_last = lambda xt: jnp.transpose(xt.reshape(b, N_HEADS, HEAD_DIM, s), (0, 3, 1, 2))
    for i in range(N_CONV_LAYERS, depth):
        jl = i - N_CONV_LAYERS
        if jl > 0:
            qa = _q_prompt(hp, norm_g[i, 0:1], w_qa[jl], ones)
        o = _flash(qa, ka, vt)
        hp = prompt_ffn(hp, o.reshape(b * s, d), i, w_o_b, jl)

    n = bd * t
    hs = x_sample.reshape(n, d)
    zeros_rows = jnp.zeros((bd, t - 2, d), _F32)
    conv_sample = []
    for i in range(N_CONV_LAYERS):
        st = state_conv[i]
        fix1 = jnp.concatenate([st[:, 1:2], st[:, 1:2], zeros_rows], axis=1).reshape(n, d)
        fix2 = jnp.concatenate([st, zeros_rows], axis=1).reshape(n, d)
        hs, cu = _conv_sample(hs, fix1, fix2, norm_g[i], w_in_b, w_conv[i], w_out_b, i)
        conv_sample.append(cu.reshape(bd, t, d)[:, t - 2:])
        hs = _ffn(hs, None, ps, i, norm_g[i], None, None, *ffn_w)
    k_s, v_s, lf_s, q_s = _kvq_sample(hs, gkv, norm_g[N_CONV_LAYERS, 0:1], w_k_b, w_v_b, w_f_pad,
                                      b_f_pad, w_qs[0])
    lf_new = jnp.pad(jnp.swapaxes(lf_s.reshape(bd, t, N_HEADS), 1, 2), ((0, 0), (0, 0), (0, PAGE - t)))
    floc_new = _page_cumsum(lf_new, triu, bd)
    k_new = k_s.reshape(bd, t, d)
    v_new = v_s.reshape(bd, t, d)
    for i in range(N_CONV_LAYERS, depth):
        jl = i - N_CONV_LAYERS
        if jl > 0:
            q_s = _q_sample(hs, norm_g[i, 0:1], w_qs[jl])
        q3 = q_s.reshape(bd, t, d)
        o = jnp.concatenate([_paged(q3[grp], *dense[g], k_new[grp], v_new[grp], floc_new[grp])
                             for g, grp in enumerate(seq_groups)], axis=0)
        hs = _ffn(hs, o.reshape(n, d), ps, i, norm_g[i], w_o_b, jl, *ffn_w)

    return (hp, hs.reshape(bd, t, d), jnp.stack(conv_prompt, axis=0),
            heads_last(kt_p), heads_last(vt_p), lf_p,
            jnp.stack(conv_sample, axis=0),
            k_s.reshape(bd, t, N_HEADS, HEAD_DIM), v_s.reshape(bd, t, N_HEADS, HEAD_DIM),
            lf_s.reshape(bd, t, N_HEADS))
```
